```python
import math
import jax, jax.numpy as jnp
from jax import lax
import numpy as np

D_MODEL = 1024
BATCH = 8
SEQ = 2048
DEPTH = 2
DEC_BATCH = 128
DEC_SEQ = 4
PAST_LEN = 2048
PAGE_SIZE = 128

N_META = 16
HG_HEADS = 8
HG_DK = D_MODEL // HG_HEADS
HG_DV = D_MODEL // HG_HEADS
HG_CHUNK = 64
AT_HEADS = 8
AT_DQK = D_MODEL // (2 * AT_HEADS)
AT_DV = D_MODEL // AT_HEADS
Q_BLOCK = 128
N_BUCKETS = 32
MAX_DISTANCE = 128
D_FF = 256 * ((8 * D_MODEL // 3 + 255) // 256)
N_EXPERTS = 8
TOP_K = 2
D_FF_EXPERT = 7 * D_MODEL // 2
N_EVEN = (DEPTH + 1) // 2
N_ODD = DEPTH // 2
LEAD = ((N_META + Q_BLOCK - 1) // Q_BLOCK) * Q_BLOCK
EPS = 1e-6
NEG = -1e30

kernel_name = 'hgrn2_diffattn_hybrid_step'


def rmsnorm(x, w):
    xf = x.astype(jnp.float32)
    y = xf * lax.rsqrt(jnp.mean(xf * xf, axis=-1, keepdims=True) + EPS)
    return (y * w.astype(jnp.float32)).astype(x.dtype)


def rel_bias(dist, table):
    n = jnp.maximum(dist, 0)
    max_exact = N_BUCKETS // 2
    nf = jnp.maximum(n, 1).astype(jnp.float32)
    large = max_exact + (jnp.log(nf / max_exact) / math.log(MAX_DISTANCE / max_exact)
                         * (N_BUCKETS - max_exact)).astype(jnp.int32)
    bucket = jnp.where(n < max_exact, n, jnp.minimum(large, N_BUCKETS - 1))
    return jnp.moveaxis(table[bucket].astype(jnp.float32), -1, 0)


def attn_partial(q, k, v, bias, mask):
    s = jnp.einsum('bqhmd,bkhmd->bhmqk', q, k).astype(jnp.float32) + bias[None, :, None]
    if mask is not None:
        s = jnp.where(mask[:, None, None], s, NEG)
    m = jnp.max(s, axis=-1)
    p = jnp.exp(s - m[..., None])
    l = jnp.sum(p, axis=-1)
    acc = jnp.einsum('bhmqk,bkhd->bhmqd', p, v.astype(jnp.float32))
    return m, l, acc


def attn_merge(a, b):
    m = jnp.maximum(a[0], b[0])
    ea = jnp.exp(a[0] - m)
    eb = jnp.exp(b[0] - m)
    return m, a[1] * ea + b[1] * eb, a[2] * ea[..., None] + b[2] * eb[..., None]


def diff_finalize(stats, lam, lambda_init, subln_w):
    _, l, acc = stats
    o = acc / l[..., None]
    o = o[:, :, 0] - lam * o[:, :, 1]
    o = rmsnorm(o, subln_w) * (1.0 - lambda_init)
    return jnp.transpose(o, (0, 2, 1, 3))


def diff_qkv(h, w_qkv):
    B, T, _ = h.shape
    q, k, v = jnp.split(h @ w_qkv, 3, axis=-1)
    q = q.reshape(B, T, AT_HEADS, 2, AT_DQK) * (AT_DQK ** -0.5)
    k = k.reshape(B, T, AT_HEADS, 2, AT_DQK)
    v = v.reshape(B, T, AT_HEADS, AT_DV)
    return q, k, v


def diff_lambda(lq1, lk1, lq2, lk2, lambda_init):
    f = jnp.float32
    return (jnp.exp(jnp.sum(lq1.astype(f) * lk1.astype(f)))
            - jnp.exp(jnp.sum(lq2.astype(f) * lk2.astype(f))) + lambda_init)


def diff_attn_prompt(h, pos, valid, w_qkv, w_out, lam, lambda_init, subln_w, table):
    B, L, _ = h.shape
    q, k, v = diff_qkv(h, w_qkv)
    nb = L // Q_BLOCK
    qb = jnp.swapaxes(q.reshape(B, nb, Q_BLOCK, AT_HEADS, 2, AT_DQK), 0, 1)
    qpos = pos.reshape(nb, Q_BLOCK)

    def block(args):
        qi, qp = args
        bias = rel_bias(qp[:, None] - pos[None, :], table)
        mask = (pos[None, :] <= qp[:, None]) & valid[None, :]
        return diff_finalize(attn_partial(qi, k, v, bias, mask[None]), lam, lambda_init, subln_w)

    o = jnp.swapaxes(lax.map(block, (qb, qpos)), 0, 1).reshape(B, L, D_MODEL)
    return o.astype(h.dtype) @ w_out, k, v


def diff_attn_sample(h, a, cache_k, cache_v, page_table, w_qkv, w_out, lam, lambda_init, subln_w, table):
    DB, T, _ = h.shape
    q, k, v = diff_qkv(h, w_qkv)
    n_pages = page_table.shape[1]
    qpos = PAST_LEN + jnp.arange(T)
    offs = jnp.arange(PAGE_SIZE)

    def page_step(carry, xs):
        pids, p = xs
        kp = cache_k[a, pids]
        vp = cache_v[a, pids]
        bias = rel_bias(qpos[:, None] - (p * PAGE_SIZE + offs)[None, :], table)
        return attn_merge(carry, attn_partial(q, kp, vp, bias, None)), None

    init = (jnp.full((DB, AT_HEADS, 2, T), NEG, jnp.float32),
            jnp.zeros((DB, AT_HEADS, 2, T), jnp.float32),
            jnp.zeros((DB, AT_HEADS, 2, T, AT_DV), jnp.float32))
    stats, _ = lax.scan(page_step, init, (page_table.T, jnp.arange(n_pages)))
    bias = rel_bias(qpos[:, None] - qpos[None, :], table)
    causal = jnp.tril(jnp.ones((T, T), bool))
    stats = attn_merge(stats, attn_partial(q, k, v, bias, causal[None]))
    o = diff_finalize(stats, lam, lambda_init, subln_w).reshape(DB, T, D_MODEL)
    return o.astype(h.dtype) @ w_out, k, v


def hgrn_proj(h, w_in, lb):
    B, T, _ = h.shape
    q, f, i, gate = jnp.split(h @ w_in, 4, axis=-1)
    forget = lb + (1.0 - lb) * jax.nn.sigmoid(f.astype(jnp.float32))
    shp = (B, T, HG_HEADS, HG_DK)
    q = jax.nn.silu(q.astype(jnp.float32)).reshape(shp)
    k = (1.0 - forget).reshape(shp)
    g = jnp.log(forget).reshape(shp)
    v = i.astype(jnp.float32).reshape(B, T, HG_HEADS, HG_DV)
    return q, k, v, g, gate


def hgrn_chunk(S, q, k, v, g):
    C = q.shape[1]
    G = jnp.cumsum(g, axis=1)
    causal = jnp.tril(jnp.ones((C, C), bool))[None, :, :, None, None]
    decay = jnp.exp(jnp.where(causal, G[:, :, None] - G[:, None, :], NEG))
    A = jnp.einsum('bthd,btshd,bshd->bhts', q, decay, k)
    o = (jnp.einsum('bhts,bshv->bthv', A, v)
         + jnp.einsum('bthd,bhdv->bthv', q * jnp.exp(G), S))
    G_end = G[:, -1]
    S_new = (jnp.exp(G_end)[..., None] * S
             + jnp.einsum('bshd,bshv->bhdv', k * jnp.exp(G_end[:, None] - G), v))
    return S_new, o


def hgrn_out(o, gate, norm_w, w_out, dtype):
    B, T = o.shape[:2]
    o = o.reshape(B, T, D_MODEL) * jax.nn.sigmoid(gate.astype(jnp.float32))
    return rmsnorm(o, norm_w).astype(dtype) @ w_out


def hgrn_prompt(h, valid, w_in, lb, norm_w, w_out):
    B, L, _ = h.shape
    q, k, v, g, gate = hgrn_proj(h, w_in, lb)
    vm = valid[None, :, None, None]
    k = jnp.where(vm, k, 0.0)
    g = jnp.where(vm, g, 0.0)
    nc = L // HG_CHUNK

    def to_chunks(t):
        return jnp.swapaxes(t.reshape(B, nc, HG_CHUNK, *t.shape[2:]), 0, 1)

    S0 = jnp.zeros((B, HG_HEADS, HG_DK, HG_DV), jnp.float32)
    S, o = lax.scan(lambda S, xs: hgrn_chunk(S, *xs), S0,
                    (to_chunks(q), to_chunks(k), to_chunks(v), to_chunks(g)))
    o = jnp.swapaxes(o, 0, 1).reshape(B, L, HG_HEADS, HG_DV)
    return hgrn_out(o, gate, norm_w, w_out, h.dtype), S


def hgrn_sample(h, S0, w_in, lb, norm_w, w_out):
    q, k, v, g, gate = hgrn_proj(h, w_in, lb)
    S, o = hgrn_chunk(S0.astype(jnp.float32), q, k, v, g)
    return hgrn_out(o, gate, norm_w, w_out, h.dtype), S


def swiglu(h, w_up, w_down):
    a, b = jnp.split(h @ w_up, 2, axis=-1)
    return (jax.nn.silu(a) * b) @ w_down


def moe_swiglu(h, w_router, b_router, w_up, w_down):
    logits = (h @ w_router).astype(jnp.float32) + b_router.astype(jnp.float32)
    top_v, top_i = lax.top_k(logits, TOP_K)
    gates = jax.nn.softmax(top_v, axis=-1)
    dense = jnp.sum(jax.nn.one_hot(top_i, N_EXPERTS, dtype=jnp.float32) * gates[..., None], axis=-2)
    out = jnp.zeros(h.shape, jnp.float32)
    for e in range(N_EXPERTS):
        out = out + dense[..., e:e + 1] * swiglu(h, w_up[e], w_down[e]).astype(jnp.float32)
    return out.astype(h.dtype)


def setup_inputs(seed: int = 0) -> dict:
    key = jax.random.key(seed)
    ks = jax.random.split(key, 32)
    f32 = jnp.float32
    n_pages = PAST_LEN // PAGE_SIZE
    n_used = DEC_BATCH * n_pages
    n_phys = n_used + max(1, n_used // 4)

    def nrm(k, shape, scale):
        return jax.random.normal(k, shape, f32) * scale

    def gain(k, shape):
        return 1.0 + 0.01 * jax.random.normal(k, shape, f32)

    page_table = jax.random.permutation(ks[5], n_phys)[:n_used].reshape(DEC_BATCH, n_pages).astype(jnp.int32)
    return {
        'x_prompt': nrm(ks[0], (BATCH, SEQ, D_MODEL), 1.0),
        'x_sample': nrm(ks[1], (DEC_BATCH, DEC_SEQ, D_MODEL), 1.0),
        'state_hgrn': nrm(ks[2], (N_EVEN, DEC_BATCH, HG_HEADS, HG_DK, HG_DV), 0.1),
        'cache_k': nrm(ks[3], (N_ODD, n_phys, PAGE_SIZE, AT_HEADS, 2, AT_DQK), 1.0),
        'cache_v': nrm(ks[4], (N_ODD, n_phys, PAGE_SIZE, AT_HEADS, AT_DV), 1.0),
        'page_table': page_table,
        'meta_tokens': nrm(ks[6], (N_META, D_MODEL), 1.0),
        'norm_mix_w': gain(ks[7], (DEPTH, D_MODEL)),
        'norm_ffn_w': gain(ks[8], (DEPTH, D_MODEL)),
        'hg_w_in': nrm(ks[9], (N_EVEN, D_MODEL, 4 * D_MODEL), D_MODEL ** -0.5),
        'hg_lower_bound': nrm(ks[10], (N_EVEN + 1, D_MODEL), 0.5),
        'hg_norm_w': gain(ks[11], (N_EVEN, D_MODEL)),
        'hg_w_out': nrm(ks[12], (N_EVEN, D_MODEL, D_MODEL), D_MODEL ** -0.5),
        'at_w_qkv': nrm(ks[13], (N_ODD, D_MODEL, 3 * D_MODEL), D_MODEL ** -0.5),
        'at_lambda_q1': nrm(ks[14], (N_ODD, AT_DQK), 0.1),
        'at_lambda_k1': nrm(ks[15], (N_ODD, AT_DQK), 0.1),
        'at_lambda_q2': nrm(ks[16], (N_ODD, AT_DQK), 0.1),
        'at_lambda_k2': nrm(ks[17], (N_ODD, AT_DQK), 0.1),
        'at_subln_w': gain(ks[18], (N_ODD, AT_DV)),
        'at_w_out': nrm(ks[19], (N_ODD, D_MODEL, D_MODEL), D_MODEL ** -0.5),
        'rel_bias_table': nrm(ks[20], (N_BUCKETS, AT_HEADS), 0.5),
        'ff_w_up': nrm(ks[21], (N_EVEN, D_MODEL, 2 * D_FF), D_MODEL ** -0.5),
        'ff_w_down': nrm(ks[22], (N_EVEN, D_FF, D_MODEL), D_FF ** -0.5),
        'moe_w_router': nrm(ks[23], (N_ODD, D_MODEL, N_EXPERTS), D_MODEL ** -0.5),
        'moe_b_router': nrm(ks[24], (N_ODD, N_EXPERTS), 0.01),
        'moe_w_up': nrm(ks[25], (N_ODD, N_EXPERTS, D_MODEL, 2 * D_FF_EXPERT), D_MODEL ** -0.5),
        'moe_w_down': nrm(ks[26], (N_ODD, N_EXPERTS, D_FF_EXPERT, D_MODEL), D_FF_EXPERT ** -0.5),
        'final_norm_w': gain(ks[27], (D_MODEL,)),
    }


def reference(x_prompt, x_sample, state_hgrn, cache_k, cache_v, page_table, meta_tokens,
              norm_mix_w, norm_ffn_w, hg_w_in, hg_lower_bound, hg_norm_w, hg_w_out,
              at_w_qkv, at_lambda_q1, at_lambda_k1, at_lambda_q2, at_lambda_k2, at_subln_w,
              at_w_out, rel_bias_table, ff_w_up, ff_w_down, moe_w_router, moe_b_router,
              moe_w_up, moe_w_down, final_norm_w):
    B, T, _ = x_prompt.shape
    pad = LEAD - N_META
    xp = jnp.concatenate([jnp.zeros((B, pad, D_MODEL), x_prompt.dtype),
                          jnp.broadcast_to(meta_tokens.astype(x_prompt.dtype), (B, N_META, D_MODEL)),
                          x_prompt], axis=1)
    pos = jnp.arange(LEAD + T) - pad
    valid = pos >= 0
    xs = x_sample
    lower_bounds = jnp.cumsum(jax.nn.softmax(hg_lower_bound.astype(jnp.float32), axis=0), axis=0)
    st_p, st_s, k_p, v_p, k_s, v_s = [], [], [], [], [], []
    for layer in range(DEPTH):
        j = layer // 2
        hp = rmsnorm(xp, norm_mix_w[layer])
        hs = rmsnorm(xs, norm_mix_w[layer])
        if layer % 2 == 0:
            mp, sp = hgrn_prompt(hp, valid, hg_w_in[j], lower_bounds[j], hg_norm_w[j], hg_w_out[j])
            ms, ss = hgrn_sample(hs, state_hgrn[j], hg_w_in[j], lower_bounds[j], hg_norm_w[j], hg_w_out[j])
            st_p.append(sp)
            st_s.append(ss)
        else:
            lambda_init = 0.8 - 0.6 * math.exp(-0.3 * layer)
            lam = diff_lambda(at_lambda_q1[j], at_lambda_k1[j], at_lambda_q2[j], at_lambda_k2[j], lambda_init)
            mp, kp, vp = diff_attn_prompt(hp, pos, valid, at_w_qkv[j], at_w_out[j], lam, lambda_init,
                                          at_subln_w[j], rel_bias_table)
            ms, kn, vn = diff_attn_sample(hs, j, cache_k, cache_v, page_table, at_w_qkv[j], at_w_out[j],
                                          lam, lambda_init, at_subln_w[j], rel_bias_table)
            k_p.append(kp[:, pad:])
            v_p.append(vp[:, pad:])
            k_s.append(kn)
            v_s.append(vn)
        xp = xp + mp
        xs = xs + ms
        hp = rmsnorm(xp, norm_ffn_w[layer])
        hs = rmsnorm(xs, norm_ffn_w[layer])
        if layer % 2 == 0:
            xp = xp + swiglu(hp, ff_w_up[j], ff_w_down[j])
            xs = xs + swiglu(hs, ff_w_up[j], ff_w_down[j])
        else:
            xp = xp + moe_swiglu(hp, moe_w_router[j], moe_b_router[j], moe_w_up[j], moe_w_down[j])
            xs = xs + moe_swiglu(hs, moe_w_router[j], moe_b_router[j], moe_w_up[j], moe_w_down[j])
    y_prompt = rmsnorm(xp, final_norm_w)[:, LEAD:]
    y_sample = rmsnorm(xs, final_norm_w)
    return (y_prompt, y_sample, jnp.stack(st_p), jnp.stack(st_s),
            jnp.stack(k_p), jnp.stack(v_p), jnp.stack(k_s), jnp.stack(v_s))
```

```python
import functools
import math

import numpy as np
import jax
import jax.numpy as jnp
from jax import lax
from jax.experimental import pallas as pl
from jax.experimental.pallas import tpu as pltpu

F32 = jnp.float32
BF16 = jnp.bfloat16

D = 1024
B = 8
SEQ = 2048
DB = 128
DT = 4
PAST = 2048
PAGE = 128
NPAGES = PAST // PAGE
NMETA = 16
LREAL = NMETA + SEQ
LP = 2176
RP = B * LP
RS = DB * DT
R = RP + RS
H = 8
HD = 128
DQK = 64
NBUCKETS = 32
MAXDIST = 128
DFF = 2816
NEXP = 8
TOPK = 2
DFFE = 3584
EPS = 1e-6
NEG = -1e30
LAMBDA_INIT = 0.8 - 0.6 * math.exp(-0.3 * 1)

VMEM_LIMIT = 56 * 1024 * 1024

TM = 512
HG_C = 128
HG_CS = 16
AT_T = 128
MOE_TM = 512
MOE_NT = (2 * R) // MOE_TM + NEXP
MOE_FC = 896
FFN_FC = 1408
PP = 4


def _cparams(sem, vmem=VMEM_LIMIT):
    return pltpu.CompilerParams(dimension_semantics=sem, vmem_limit_bytes=vmem)


def _rms(x, w):
    return (x * lax.rsqrt(jnp.mean(x * x, axis=-1, keepdims=True) + EPS)) * w


def _norm_matmul_kernel(x_ref, wn_ref, w_ref, o_ref, h_ref):
    @pl.when(pl.program_id(1) == 0)
    def _():
        h_ref[...] = _rms(x_ref[...], wn_ref[...]).astype(BF16)

    o_ref[...] = jnp.dot(h_ref[...], w_ref[...], preferred_element_type=F32)


def norm_matmul(x, wn, w, tn):
    rows, n = x.shape[0], w.shape[1]
    return pl.pallas_call(
        _norm_matmul_kernel,
        grid=(rows // TM, n // tn),
        in_specs=[pl.BlockSpec((TM, D), lambda i, j: (i, 0)),
                  pl.BlockSpec((1, D), lambda i, j: (0, 0)),
                  pl.BlockSpec((D, tn), lambda i, j: (0, j))],
        out_specs=pl.BlockSpec((TM, tn), lambda i, j: (i, j)),
        out_shape=jax.ShapeDtypeStruct((rows, n), F32),
        scratch_shapes=[pltpu.VMEM((TM, D), BF16)],
        compiler_params=_cparams(("parallel", "arbitrary")),
        name="norm_matmul",
    )(x, wn.reshape(1, D), w)


def _matmul_residual_kernel(a_ref, w_ref, x_ref, o_ref):
    o_ref[...] = x_ref[...] + jnp.dot(a_ref[...].astype(BF16), w_ref[...],
                                      preferred_element_type=F32)


def matmul_residual(a, w, x):
    rows = x.shape[0]
    return pl.pallas_call(
        _matmul_residual_kernel,
        grid=(rows // TM,),
        in_specs=[pl.BlockSpec((TM, D), lambda i: (i, 0)),
                  pl.BlockSpec((D, D), lambda i: (0, 0)),
                  pl.BlockSpec((TM, D), lambda i: (i, 0))],
        out_specs=pl.BlockSpec((TM, D), lambda i: (i, 0)),
        out_shape=jax.ShapeDtypeStruct((rows, D), F32),
        compiler_params=_cparams(("parallel",)),
        name="matmul_residual",
    )(a, w, x)


def _ffn_kernel(x_ref, wn_ref, wa_ref, wb_ref, wd_ref, o_ref, h_ref):
    j = pl.program_id(1)

    @pl.when(j == 0)
    def _():
        h_ref[...] = _rms(x_ref[...], wn_ref[...]).astype(BF16)

    h = h_ref[...]
    a = jnp.dot(h, wa_ref[...], preferred_element_type=F32)
    b = jnp.dot(h, wb_ref[...], preferred_element_type=F32)
    act = (a * jax.nn.sigmoid(a) * b).astype(BF16)
    y = jnp.dot(act, wd_ref[...], preferred_element_type=F32)

    @pl.when(j == 0)
    def _():
        o_ref[...] = x_ref[...] + y

    @pl.when(j > 0)
    def _():
        o_ref[...] += y


def dense_ffn(x, wn, w_up, w_down):
    rows = x.shape[0]
    nf = DFF // FFN_FC
    return pl.pallas_call(
        _ffn_kernel,
        grid=(rows // TM, nf),
        in_specs=[pl.BlockSpec((TM, D), lambda i, j: (i, 0)),
                  pl.BlockSpec((1, D), lambda i, j: (0, 0)),
                  pl.BlockSpec((D, FFN_FC), lambda i, j: (0, j)),
                  pl.BlockSpec((D, FFN_FC), lambda i, j: (0, nf + j)),
                  pl.BlockSpec((FFN_FC, D), lambda i, j: (j, 0))],
        out_specs=pl.BlockSpec((TM, D), lambda i, j: (i, 0)),
        out_shape=jax.ShapeDtypeStruct((rows, D), F32),
        scratch_shapes=[pltpu.VMEM((TM, D), BF16)],
        compiler_params=_cparams(("parallel", "arbitrary")),
        name="dense_ffn",
    )(x, wn.reshape(1, D), w_up, w_up, w_down)


def _hgrn_masks(C):
    nlev = int(math.log2(C))
    t = np.arange(C)[:, None]
    s = np.arange(C)[None, :]
    m = [(t == s)]
    for lev in range(nlev):
        b = 1 << lev
        m.append((t // (2 * b) == s // (2 * b)) & ((t // b) % 2 == 1) & ((s // b) % 2 == 0))
    return np.stack(m).astype(np.float32)


def _hgrn_kernel(proj_ref, lb_ref, nw_ref, msk_ref, s0_ref, y_ref, s_ref, *, C, n_valid):
    nlev = int(math.log2(C))
    c = pl.program_id(1)

    @pl.when(c == 0)
    def _():
        s_ref[...] = s0_ref[...]

    rloc = lax.broadcasted_iota(jnp.int32, (C, HD), 0)
    valid = (c * C + rloc) < n_valid
    nt = (((1,), (1,)), ((), ()))
    tn = (((0,), (0,)), ((), ()))
    o_heads = []
    for h in range(H):
        col = slice(h * HD, (h + 1) * HD)
        lb = lb_ref[:, col]
        q = proj_ref[:, h * HD:(h + 1) * HD]
        f = proj_ref[:, D + h * HD:D + (h + 1) * HD]
        v = proj_ref[:, 2 * D + h * HD:2 * D + (h + 1) * HD]
        qs = q * jax.nn.sigmoid(q)
        fg = lb + (1.0 - lb) * jax.nn.sigmoid(f)
        k = jnp.where(valid, 1.0 - fg, 0.0)
        g = jnp.where(valid, jnp.log(fg), 0.0)
        G = g
        sh = 1
        while sh < C:
            G = G + jnp.where(rloc >= sh, pltpu.roll(G, sh, 0), 0.0)
            sh *= 2
        vb = v.astype(BF16)
        A = msk_ref[0] * lax.dot_general(qs.astype(BF16), k.astype(BF16), nt,
                                         preferred_element_type=F32)
        for lev in range(nlev):
            b = 1 << lev
            if b >= 8:
                gm = G.reshape(C // (2 * b), 2 * b, HD)[:, b - 1:b, :]
                gmid = jnp.broadcast_to(gm, (C // (2 * b), 2 * b, HD)).reshape(C, HD)
            else:
                g8 = G.reshape(C // 8, 8, HD)
                sub = lax.broadcasted_iota(jnp.int32, (C // 8, 8, HD), 1)
                gmid = None
                for base in range(0, 8, 2 * b):
                    row = jnp.broadcast_to(g8[:, base + b - 1:base + b, :], (C // 8, 8, HD))
                    gmid = row if gmid is None else jnp.where(sub >= base, row, gmid)
                gmid = gmid.reshape(C, HD)
            right = (rloc // b) % 2 == 1
            fac = jnp.exp(jnp.where(right, G - gmid, gmid - G))
            qt = jnp.where(right, qs * fac, 0.0).astype(BF16)
            kt = jnp.where(right, 0.0, k * fac).astype(BF16)
            A = A + msk_ref[lev + 1] * lax.dot_general(qt, kt, nt, preferred_element_type=F32)
        g_end = G[C - 1:C, :]
        qh = (qs * jnp.exp(G)).astype(BF16)
        kh = (k * jnp.exp(g_end - G)).astype(BF16)
        S = s_ref[0, h]
        o_h = (jnp.dot(A.astype(BF16), vb, preferred_element_type=F32)
               + jnp.dot(qh, S.astype(BF16), preferred_element_type=F32))
        e_col = jnp.broadcast_to(jnp.exp(g_end), (HD, HD)).T
        s_ref[0, h] = e_col * S + lax.dot_general(kh, vb, tn, preferred_element_type=F32)
        o_heads.append(o_h)
    o = jnp.concatenate(o_heads, axis=1)
    gate = proj_ref[:, 3 * D:4 * D]
    y_ref[...] = _rms(o * jax.nn.sigmoid(gate), nw_ref[...]).astype(BF16)


def hgrn_mixer(proj, lb, nw, s0, nb, C, n_chunks, n_valid, s0_per_batch):
    msk = jnp.asarray(_hgrn_masks(C))
    nlev1 = msk.shape[0]
    s0_map = (lambda b, c: (b, 0, 0, 0)) if s0_per_batch else (lambda b, c: (0, 0, 0, 0))
    return pl.pallas_call(
        functools.partial(_hgrn_kernel, C=C, n_valid=n_valid),
        grid=(nb, n_chunks),
        in_specs=[pl.BlockSpec((C, 4 * D), lambda b, c: (b * n_chunks + c, 0)),
                  pl.BlockSpec((1, D), lambda b, c: (0, 0)),
                  pl.BlockSpec((1, D), lambda b, c: (0, 0)),
                  pl.BlockSpec((nlev1, C, C), lambda b, c: (0, 0, 0)),
                  pl.BlockSpec((1, H, HD, HD), s0_map)],
        out_specs=[pl.BlockSpec((C, D), lambda b, c: (b * n_chunks + c, 0)),
                   pl.BlockSpec((1, H, HD, HD), lambda b, c: (b, 0, 0, 0))],
        out_shape=[jax.ShapeDtypeStruct((nb * n_chunks * C, D), BF16),
                   jax.ShapeDtypeStruct((nb, H, HD, HD), F32)],
        compiler_params=_cparams(("parallel", "arbitrary")),
        name="hgrn_mixer",
    )(proj, lb.reshape(1, D), nw.reshape(1, D), msk, s0)


def _bias_by_distance(dist, table):
    n = jnp.maximum(dist, 0)
    max_exact = NBUCKETS // 2
    nf = jnp.maximum(n, 1).astype(F32)
    large = max_exact + (jnp.log(nf / max_exact) / math.log(MAXDIST / max_exact)
                         * (NBUCKETS - max_exact)).astype(jnp.int32)
    bucket = jnp.where(n < max_exact, n, jnp.minimum(large, NBUCKETS - 1))
    return jnp.moveaxis(table[bucket].astype(F32), -1, 0)


def _subln(d, w):
    return _rms(d, w) * (1.0 - LAMBDA_INIT)


def _attn_prompt_kernel(cfar_ref, lam_ref, q_ref, k_ref, v_ref, strip_ref, sw_ref, o_ref, s_scr):
    T = AT_T
    h = pl.program_id(1)
    i = pl.program_id(2)
    nt = (((1,), (1,)), ((), ()))
    qb = q_ref[0, 0]
    lane = lax.broadcasted_iota(jnp.int32, (T, HD), 1)
    zero = jnp.zeros_like(qb)
    qq = jnp.concatenate([jnp.where(lane < DQK, qb, zero), jnp.where(lane >= DQK, qb, zero)], axis=0)

    def scores(c):
        kc = k_ref[0, 0, pl.ds(pl.multiple_of(c * T, T), T), :]
        return lax.dot_general(qq, kc, nt, preferred_element_type=F32)

    cfar = cfar_ref[h]

    def far_body(c, mx):
        s = scores(c) + cfar
        s_scr[c] = s
        return jnp.maximum(mx, s)

    mx = lax.fori_loop(0, jnp.maximum(i - 1, 0), far_body, jnp.full((2 * T, T), NEG, F32))

    def near(c, which, mx):
        st = strip_ref[0, which]
        s = scores(c) + jnp.concatenate([st, st], axis=0)
        s_scr[c] = s
        return jnp.maximum(mx, s)

    mx = near(jnp.maximum(i - 1, 0), 1, mx)
    mx = near(i, 0, mx)
    m = jnp.max(mx, axis=-1, keepdims=True)

    def pv_body(c, carry):
        acc, ls = carry
        p = jnp.exp(s_scr[c] - m)
        vc = v_ref[0, 0, pl.ds(pl.multiple_of(c * T, T), T), :]
        return acc + jnp.dot(p.astype(BF16), vc, preferred_element_type=F32), ls + p

    acc, ls = lax.fori_loop(0, i + 1, pv_body,
                            (jnp.zeros((2 * T, HD), F32), jnp.zeros((2 * T, T), F32)))
    o = acc / jnp.sum(ls, axis=-1, keepdims=True)
    o_ref[...] = _subln(o[:T] - lam_ref[0] * o[T:], sw_ref[...])


def attn_prompt(q, k, v, strips, cfar, lam, sw):
    T = AT_T
    nq = LP // T
    smem = pl.BlockSpec(memory_space=pltpu.SMEM)
    return pl.pallas_call(
        _attn_prompt_kernel,
        grid=(B, H, nq),
        in_specs=[smem, smem,
                  pl.BlockSpec((1, 1, T, HD), lambda b, h, i: (b, h, i, 0)),
                  pl.BlockSpec((1, 1, LP, HD), lambda b, h, i: (b, h, 0, 0)),
                  pl.BlockSpec((1, 1, LP, HD), lambda b, h, i: (b, h, 0, 0)),
                  pl.BlockSpec((1, 2, T, T), lambda b, h, i: (h, 0, 0, 0)),
                  pl.BlockSpec((1, HD), lambda b, h, i: (0, 0))],
        out_specs=pl.BlockSpec((T, HD), lambda b, h, i: (b * nq + i, h)),
        out_shape=jax.ShapeDtypeStruct((B * LP, D), F32),
        scratch_shapes=[pltpu.VMEM((nq, 2 * T, T), F32)],
        compiler_params=_cparams(("parallel", "parallel", "arbitrary")),
        name="attn_prompt",
    )(cfar, lam, q, k, v, strips, sw.reshape(1, HD))


def _attn_decode_kernel(pt_ref, lam_ref, q_ref, *refs):
    del pt_ref
    k_refs = refs[:PP]
    v_refs = refs[PP:2 * PP]
    bias_ref, kn_ref, vn_ref, bn_ref, sw_ref, o_ref, m_ref, l_ref, acc_ref = refs[2 * PP:]
    g = pl.program_id(1)

    @pl.when(g == 0)
    def _():
        m_ref[...] = jnp.full(m_ref.shape, NEG, F32)
        l_ref[...] = jnp.zeros(l_ref.shape, F32)
        acc_ref[...] = jnp.zeros(acc_ref.shape, F32)

    qblk = q_ref[0]

    def lanes_to_rows(x):
        return jnp.broadcast_to(x, (HD, HD)).T

    def update(kp, vp, bias):
        s = jnp.dot(kp.astype(BF16), qblk, preferred_element_type=F32) + bias
        m_old = m_ref[...]
        m_new = jnp.maximum(m_old, jnp.max(s, axis=0, keepdims=True))
        alpha = jnp.exp(m_old - m_new)
        p = jnp.exp(s - m_new)
        l_ref[...] = alpha * l_ref[...] + jnp.sum(p, axis=0, keepdims=True)
        m_ref[...] = m_new
        pv = jnp.dot(p.T.astype(BF16), vp.astype(BF16), preferred_element_type=F32)
        a_rows = lanes_to_rows(alpha)
        for hh in range(H):
            col = slice(hh * HD, (hh + 1) * HD)
            acc_ref[:, col] = acc_ref[:, col] * a_rows + pv[:, col]

    for j in range(PP):
        update(k_refs[j][0], v_refs[j][0], bias_ref[j])

    @pl.when(g == pl.num_programs(1) - 1)
    def _():
        update(kn_ref[0], vn_ref[0], bn_ref[...])
        inv_l = lanes_to_rows(1.0 / l_ref[...])
        outs = []
        for hh in range(H):
            col = slice(hh * HD, (hh + 1) * HD)
            r1 = slice(hh * 8, hh * 8 + 8)
            r2 = slice(64 + hh * 8, 64 + hh * 8 + 8)
            o1 = acc_ref[r1, col] * inv_l[r1, :]
            o2 = acc_ref[r2, col] * inv_l[r2, :]
            outs.append(_subln(o1 - lam_ref[0] * o2, sw_ref[...]))
        o_ref[0] = jnp.concatenate(outs, axis=1)


def attn_decode(page_table, lam, qblk, cache_k, cache_v, bias_pages, k_new, v_new, bias_new, sw):
    n_steps = NPAGES // PP

    def page_spec(j):
        return pl.BlockSpec((1, PAGE, D),
                            lambda b, g, pt: (pt[b * NPAGES + g * PP + j], 0, 0))

    smem = pl.BlockSpec(memory_space=pltpu.SMEM)
    grid_spec = pltpu.PrefetchScalarGridSpec(
        num_scalar_prefetch=1,
        grid=(DB, n_steps),
        in_specs=([smem, pl.BlockSpec((1, D, HD), lambda b, g, pt: (b, 0, 0))]
                  + [page_spec(j) for j in range(PP)]
                  + [page_spec(j) for j in range(PP)]
                  + [pl.BlockSpec((PP, PAGE, HD), lambda b, g, pt: (g, 0, 0)),
                     pl.BlockSpec((1, 16, D), lambda b, g, pt: (b, 0, 0)),
                     pl.BlockSpec((1, 16, D), lambda b, g, pt: (b, 0, 0)),
                     pl.BlockSpec((16, HD), lambda b, g, pt: (0, 0)),
                     pl.BlockSpec((1, HD), lambda b, g, pt: (0, 0))]),
        out_specs=pl.BlockSpec((1, 8, D), lambda b, g, pt: (b, 0, 0)),
        scratch_shapes=[pltpu.VMEM((1, HD), F32), pltpu.VMEM((1, HD), F32),
                        pltpu.VMEM((HD, D), F32)],
    )
    return pl.pallas_call(
        _attn_decode_kernel,
        grid_spec=grid_spec,
        out_shape=jax.ShapeDtypeStruct((DB, 8, D), F32),
        compiler_params=_cparams(("parallel", "arbitrary")),
        name="attn_decode",
    )(page_table.reshape(-1), lam, qblk, *([cache_k] * PP), *([cache_v] * PP),
      bias_pages, k_new, v_new, bias_new, sw.reshape(1, HD))


def _router_kernel(x_ref, wn_ref, wr_ref, br_ref, h_ref, idx_ref, gate_ref):
    h = _rms(x_ref[...], wn_ref[...])
    h_ref[...] = h
    h_hi = h.astype(BF16)
    h_lo = (h - h_hi.astype(F32)).astype(BF16)
    logits = (jnp.dot(h_hi, wr_ref[0], preferred_element_type=F32)
              + jnp.dot(h_lo, wr_ref[0], preferred_element_type=F32)
              + jnp.dot(h_hi, wr_ref[1], preferred_element_type=F32)) + br_ref[...]
    lane = lax.broadcasted_iota(jnp.int32, logits.shape, 1)
    ninf = jnp.float32(-jnp.inf)
    logits = jnp.where(lane < NEXP, logits, ninf)
    v1 = jnp.max(logits, axis=-1, keepdims=True)
    i1 = jnp.min(jnp.where(logits == v1, lane, HD), axis=-1, keepdims=True)
    rest = jnp.where(lane == i1, ninf, logits)
    v2 = jnp.max(rest, axis=-1, keepdims=True)
    i2 = jnp.min(jnp.where(rest == v2, lane, HD), axis=-1, keepdims=True)
    e = jnp.exp(v2 - v1)
    g1 = 1.0 / (1.0 + e)
    g2 = e / (1.0 + e)
    idx_ref[...] = jnp.where(lane == 0, i1, jnp.where(lane == 1, i2, 0))
    gate_ref[...] = jnp.where(lane == 0, g1, jnp.where(lane == 1, g2, 0.0))


def moe_router(x, wn, w_router, b_router):
    rows = x.shape[0]
    wr = jnp.zeros((D, HD), F32).at[:, :NEXP].set(w_router.astype(F32))
    wr_hi = wr.astype(BF16)
    wr_lo = (wr - wr_hi.astype(F32)).astype(BF16)
    br = jnp.zeros((1, HD), F32).at[0, :NEXP].set(b_router.astype(F32))
    return pl.pallas_call(
        _router_kernel,
        grid=(rows // TM,),
        in_specs=[pl.BlockSpec((TM, D), lambda i: (i, 0)),
                  pl.BlockSpec((1, D), lambda i: (0, 0)),
                  pl.BlockSpec((2, D, HD), lambda i: (0, 0, 0)),
                  pl.BlockSpec((1, HD), lambda i: (0, 0))],
        out_specs=[pl.BlockSpec((TM, D), lambda i: (i, 0)),
                   pl.BlockSpec((TM, HD), lambda i: (i, 0)),
                   pl.BlockSpec((TM, HD), lambda i: (i, 0))],
        out_shape=[jax.ShapeDtypeStruct((rows, D), F32),
                   jax.ShapeDtypeStruct((rows, HD), jnp.int32),
                   jax.ShapeDtypeStruct((rows, HD), F32)],
        compiler_params=_cparams(("parallel",)),
        name="moe_router",
    )(x, wn.reshape(1, D), jnp.stack([wr_hi, wr_lo]), br)


def _row_copy(src_hbm, row, dst, r, sem):
    return pltpu.make_async_copy(src_hbm.at[pl.ds(row, 1)], dst.at[pl.ds(r, 1)], sem)


def _moe_kernel(te_ref, tv_ref, src_ref, h_hbm, wa_ref, wb_ref, wd_ref, g_ref, o_ref,
                hbuf, hb16, sem):
    del te_ref
    i = pl.program_id(0)
    j = pl.program_id(1)
    live = tv_ref[i] == 1

    @pl.when(live & (j == 0))
    def _():
        def start(r, carry):
            _row_copy(h_hbm, src_ref[i * MOE_TM + r], hbuf, r, sem).start()
            return carry

        def wait(r, carry):
            _row_copy(h_hbm, 0, hbuf, r, sem).wait()
            return carry

        lax.fori_loop(0, MOE_TM, start, 0)
        lax.fori_loop(0, MOE_TM, wait, 0)
        hb16[...] = hbuf[...].astype(BF16)

    @pl.when(live)
    def _():
        h = hb16[...]
        a = jnp.dot(h, wa_ref[0], preferred_element_type=F32)
        b = jnp.dot(h, wb_ref[0], preferred_element_type=F32)
        act = (a * jax.nn.sigmoid(a) * b).astype(BF16)
        y = jnp.dot(act, wd_ref[0], preferred_element_type=F32)

        @pl.when(j == 0)
        def _():
            o_ref[...] = y

        @pl.when(j > 0)
        def _():
            o_ref[...] += y

        @pl.when(j == pl.num_programs(1) - 1)
        def _():
            o_ref[...] = o_ref[...] * g_ref[...]

    @pl.when(jnp.logical_not(live) & (j == 0))
    def _():
        o_ref[...] = jnp.zeros(o_ref.shape, F32)


def moe_experts(tile_expert, tile_valid, src_row, h, w_up, w_down, slot_gate):
    nf = DFFE // MOE_FC
    grid_spec = pltpu.PrefetchScalarGridSpec(
        num_scalar_prefetch=3,
        grid=(MOE_NT, nf),
        in_specs=[pl.BlockSpec(memory_space=pl.ANY),
                  pl.BlockSpec((1, D, MOE_FC), lambda i, j, te, tv, sr: (te[i], 0, j)),
                  pl.BlockSpec((1, D, MOE_FC), lambda i, j, te, tv, sr: (te[i], 0, nf + j)),
                  pl.BlockSpec((1, MOE_FC, D), lambda i, j, te, tv, sr: (te[i], j, 0)),
                  pl.BlockSpec((MOE_TM, 1), lambda i, j, te, tv, sr: (i, 0))],
        out_specs=pl.BlockSpec((MOE_TM, D), lambda i, j, te, tv, sr: (i, 0)),
        scratch_shapes=[pltpu.VMEM((MOE_TM, D), F32), pltpu.VMEM((MOE_TM, D), BF16),
                        pltpu.SemaphoreType.DMA],
    )
    return pl.pallas_call(
        _moe_kernel,
        grid_spec=grid_spec,
        out_shape=jax.ShapeDtypeStruct((MOE_NT * MOE_TM, D), F32),
        compiler_params=_cparams(("arbitrary", "arbitrary")),
        name="moe_experts",
    )(tile_expert, tile_valid, src_row, h, w_up, w_up, w_down, slot_gate)


def _combine_kernel(pos_ref, x_ref, eo_hbm, wn_ref, y_ref, b0, b1, sem):
    i = pl.program_id(0)

    def start(r, carry):
        base = 2 * (i * TM + r)
        _row_copy(eo_hbm, pos_ref[base], b0, r, sem).start()
        _row_copy(eo_hbm, pos_ref[base + 1], b1, r, sem).start()
        return carry

    def wait(r, carry):
        _row_copy(eo_hbm, 0, b0, r, sem).wait()
        _row_copy(eo_hbm, 0, b1, r, sem).wait()
        return carry

    lax.fori_loop(0, TM, start, 0)
    lax.fori_loop(0, TM, wait, 0)
    y_ref[...] = _rms(x_ref[...] + (b0[...] + b1[...]), wn_ref[...])


def moe_combine_norm(pos, x, eo, wn):
    rows = x.shape[0]
    grid_spec = pltpu.PrefetchScalarGridSpec(
        num_scalar_prefetch=1,
        grid=(rows // TM,),
        in_specs=[pl.BlockSpec((TM, D), lambda i, p: (i, 0)),
                  pl.BlockSpec(memory_space=pl.ANY),
                  pl.BlockSpec((1, D), lambda i, p: (0, 0))],
        out_specs=pl.BlockSpec((TM, D), lambda i, p: (i, 0)),
        scratch_shapes=[pltpu.VMEM((TM, D), F32), pltpu.VMEM((TM, D), F32),
                        pltpu.SemaphoreType.DMA],
    )
    return pl.pallas_call(
        _combine_kernel,
        grid_spec=grid_spec,
        out_shape=jax.ShapeDtypeStruct((rows, D), F32),
        compiler_params=_cparams(("arbitrary",)),
        name="moe_combine_norm",
    )(pos, x, eo, wn.reshape(1, D))


def _moe_plan(idx):
    e_flat = idx.reshape(-1)
    onehot = (e_flat[:, None] == jnp.arange(NEXP)[None, :]).astype(jnp.int32)
    rank = jnp.take_along_axis(jnp.cumsum(onehot, axis=0) - onehot, e_flat[:, None], axis=1)[:, 0]
    counts = jnp.sum(onehot, axis=0)
    padded = ((counts + MOE_TM - 1) // MOE_TM) * MOE_TM
    ends = jnp.cumsum(padded)
    slot = (ends - padded)[e_flat] + rank
    n_slots = MOE_NT * MOE_TM
    src_row = jnp.zeros((n_slots,), jnp.int32).at[slot].set(jnp.arange(2 * R, dtype=jnp.int32) // 2)
    tile_start = jnp.arange(MOE_NT, dtype=jnp.int32) * MOE_TM
    tile_valid = (tile_start < ends[-1]).astype(jnp.int32)
    tile_expert = jnp.minimum(jnp.searchsorted(ends, tile_start, side="right"), NEXP - 1).astype(jnp.int32)
    last = jnp.maximum(jnp.sum(tile_valid) - 1, 0)
    tile_expert = jnp.where(tile_valid == 1, tile_expert, tile_expert[last])
    return slot.astype(jnp.int32), src_row, tile_expert, tile_valid


def kernel(x_prompt, x_sample, state_hgrn, cache_k, cache_v, page_table, meta_tokens,
           norm_mix_w, norm_ffn_w, hg_w_in, hg_lower_bound, hg_norm_w, hg_w_out,
           at_w_qkv, at_lambda_q1, at_lambda_k1, at_lambda_q2, at_lambda_k2, at_subln_w,
           at_w_out, rel_bias_table, ff_w_up, ff_w_down, moe_w_router, moe_b_router,
           moe_w_up, moe_w_down, final_norm_w):
    xp = jnp.concatenate([jnp.broadcast_to(meta_tokens.astype(F32), (B, NMETA, D)), x_prompt,
                          jnp.zeros((B, LP - LREAL, D), F32)], axis=1)
    x0 = jnp.concatenate([xp.reshape(RP, D), x_sample.reshape(RS, D)], axis=0)
    lower = jnp.cumsum(jax.nn.softmax(hg_lower_bound.astype(F32), axis=0), axis=0)

    proj = norm_matmul(x0, norm_mix_w[0], hg_w_in[0].astype(BF16), 1024)
    y_p, st_p = hgrn_mixer(proj, lower[0], hg_norm_w[0], jnp.zeros((1, H, HD, HD), F32),
                           B, HG_C, LP // HG_C, LREAL, False)
    proj_s = jnp.pad(proj[RP:].reshape(DB, DT, 4 * D), ((0, 0), (0, HG_CS - DT), (0, 0)))
    y_s, st_s = hgrn_mixer(proj_s.reshape(DB * HG_CS, 4 * D), lower[0], hg_norm_w[0], state_hgrn[0],
                           DB, HG_CS, 1, DT, True)
    y0 = jnp.concatenate([y_p, y_s.reshape(DB, HG_CS, D)[:, :DT].reshape(RS, D)], axis=0)
    x1 = matmul_residual(y0, hg_w_out[0].astype(BF16), x0)
    x2 = dense_ffn(x1, norm_ffn_w[0], ff_w_up[0].astype(BF16), ff_w_down[0].astype(BF16))

    f = F32
    lam = (jnp.exp(jnp.sum(at_lambda_q1[0].astype(f) * at_lambda_k1[0].astype(f)))
           - jnp.exp(jnp.sum(at_lambda_q2[0].astype(f) * at_lambda_k2[0].astype(f))) + LAMBDA_INIT)
    lam = lam.reshape(1).astype(F32)
    qkv = norm_matmul(x2, norm_mix_w[1], at_w_qkv[0].astype(BF16), 1024)
    q_all, k_all, v_all = qkv[:, :D] * (DQK ** -0.5), qkv[:, D:2 * D], qkv[:, 2 * D:]

    def heads_major(t):
        return t[:RP].reshape(B, LP, H, HD).transpose(0, 2, 1, 3).astype(BF16)

    ar = jnp.arange(AT_T)
    dist0 = ar[:, None] - ar[None, :]
    strip0 = jnp.where(dist0 >= 0, _bias_by_distance(dist0, rel_bias_table), NEG)
    strip1 = _bias_by_distance(dist0 + AT_T, rel_bias_table)
    strips = jnp.stack([strip0, strip1], axis=1)
    cfar = rel_bias_table[NBUCKETS - 1].astype(F32)
    o_p = attn_prompt(heads_major(q_all), heads_major(k_all), heads_major(v_all),
                      strips, cfar, lam, at_subln_w[0])

    q_s = q_all[RP:].reshape(DB, DT, H, 2, DQK)
    sel = jnp.einsum("hj,mn->hmnj", jnp.eye(H, dtype=F32), jnp.eye(2, dtype=F32))
    qblk = jnp.einsum("bthmd,hmnj->bhmdnjt", q_s, sel)
    qblk = jnp.pad(qblk, ((0, 0),) * 6 + ((0, 8 - DT),)).reshape(DB, D, HD).astype(BF16)
    tq = jnp.minimum(jnp.arange(8), DT - 1)
    kpos = jnp.arange(PAST).reshape(NPAGES, PAGE)
    bp = _bias_by_distance(PAST + tq[None, None, :] - kpos[:, :, None], rel_bias_table)
    bias_pages = jnp.broadcast_to(bp.transpose(1, 2, 0, 3)[:, :, None], (NPAGES, PAGE, 2, H, 8))
    bias_pages = bias_pages.reshape(NPAGES, PAGE, HD)
    sn = jnp.arange(16)
    dn = tq[None, :] - sn[:, None]
    bn = jnp.where((dn >= 0) & (sn[:, None] < DT), _bias_by_distance(dn, rel_bias_table), NEG)
    bias_new = jnp.broadcast_to(bn.transpose(1, 0, 2)[:, None], (16, 2, H, 8)).reshape(16, HD)
    k_new = jnp.pad(k_all[RP:].reshape(DB, DT, D), ((0, 0), (0, 16 - DT), (0, 0)))
    v_new = jnp.pad(v_all[RP:].reshape(DB, DT, D), ((0, 0), (0, 16 - DT), (0, 0)))
    n_phys = cache_k.shape[1]
    o_s = attn_decode(page_table, lam, qblk, cache_k[0].reshape(n_phys, PAGE, D),
                      cache_v[0].reshape(n_phys, PAGE, D), bias_pages, k_new, v_new, bias_new,
                      at_subln_w[0])
    o1 = jnp.concatenate([o_p, o_s[:, :DT].reshape(RS, D)], axis=0)
    x3 = matmul_residual(o1, at_w_out[0].astype(BF16), x2)

    h_moe, idx, gates = moe_router(x3, norm_ffn_w[1], moe_w_router[0], moe_b_router[0])
    slot, src_row, tile_expert, tile_valid = _moe_plan(idx[:, :TOPK])
    slot_gate = jnp.zeros((MOE_NT * MOE_TM, 1), F32).at[slot, 0].set(gates[:, :TOPK].reshape(-1))
    eo = moe_experts(tile_expert, tile_valid, src_row, h_moe,
                     moe_w_up[0].astype(BF16), moe_w_down[0].astype(BF16), slot_gate)
    y = moe_combine_norm(slot, x3, eo, final_norm_w)

    y_prompt = y[:RP].reshape(B, LP, D)[:, NMETA:LREAL]
    y_sample = y[RP:].reshape(DB, DT, D)
    k_p = k_all[:RP].reshape(B, LP, H, 2, DQK)[:, :LREAL]
    v_p = v_all[:RP].reshape(B, LP, H, HD)[:, :LREAL]
    k_s = k_all[RP:].reshape(DB, DT, H, 2, DQK)
    v_s = v_all[RP:].reshape(DB, DT, H, HD)
    return (y_prompt, y_sample, st_p[None], st_s[None], k_p[None], v_p[None], k_s[None], v_s[None])
```

```python
import functools
import math

import numpy as np
import jax
import jax.numpy as jnp
from jax import lax
from jax.experimental import pallas as pl
from jax.experimental.pallas import tpu as pltpu

F32 = jnp.float32
BF16 = jnp.bfloat16

D = 1024
B = 8
SEQ = 2048
DB = 128
DT = 4
PAST = 2048
PAGE = 128
NPAGES = PAST // PAGE
NMETA = 16
LREAL = NMETA + SEQ
LP = 2176
RP = B * LP
RS = DB * DT
R = RP + RS
H = 8
HD = 128
DQK = 64
NBUCKETS = 32
MAXDIST = 128
DFF = 2816
NEXP = 8
TOPK = 2
DFFE = 3584
EPS = 1e-6
NEG = -1e30
LAMBDA_INIT = 0.8 - 0.6 * math.exp(-0.3 * 1)

VMEM_LIMIT = 56 * 1024 * 1024

TM = 512
HG_C = 128
HG_CS = 16
AT_T = 256
LA = 2304
MOE_TM = 512
MOE_NT = (2 * R) // MOE_TM + NEXP
MOE_FC = 1792
FFN_FC = 1408
PP = 4


def _cparams(sem, vmem=VMEM_LIMIT):
    return pltpu.CompilerParams(dimension_semantics=sem, vmem_limit_bytes=vmem)


def _rms(x, w):
    return (x * lax.rsqrt(jnp.mean(x * x, axis=-1, keepdims=True) + EPS)) * w


def _norm_matmul_kernel(x_ref, wn_ref, w_ref, o_ref, h_ref):
    @pl.when(pl.program_id(1) == 0)
    def _():
        h_ref[...] = _rms(x_ref[...], wn_ref[...]).astype(BF16)

    o_ref[...] = jnp.dot(h_ref[...], w_ref[...], preferred_element_type=F32)


def norm_matmul(x, wn, w, tn):
    rows, n = x.shape[0], w.shape[1]
    return pl.pallas_call(
        _norm_matmul_kernel,
        grid=(rows // TM, n // tn),
        in_specs=[pl.BlockSpec((TM, D), lambda i, j: (i, 0)),
                  pl.BlockSpec((1, D), lambda i, j: (0, 0)),
                  pl.BlockSpec((D, tn), lambda i, j: (0, j))],
        out_specs=pl.BlockSpec((TM, tn), lambda i, j: (i, j)),
        out_shape=jax.ShapeDtypeStruct((rows, n), F32),
        scratch_shapes=[pltpu.VMEM((TM, D), BF16)],
        compiler_params=_cparams(("parallel", "arbitrary")),
        name="norm_matmul",
    )(x, wn.reshape(1, D), w)


def _matmul_residual_kernel(a_ref, w_ref, x_ref, o_ref):
    o_ref[...] = x_ref[...] + jnp.dot(a_ref[...].astype(BF16), w_ref[...],
                                      preferred_element_type=F32)


def matmul_residual(a, w, x):
    rows = x.shape[0]
    return pl.pallas_call(
        _matmul_residual_kernel,
        grid=(rows // TM,),
        in_specs=[pl.BlockSpec((TM, D), lambda i: (i, 0)),
                  pl.BlockSpec((D, D), lambda i: (0, 0)),
                  pl.BlockSpec((TM, D), lambda i: (i, 0))],
        out_specs=pl.BlockSpec((TM, D), lambda i: (i, 0)),
        out_shape=jax.ShapeDtypeStruct((rows, D), F32),
        compiler_params=_cparams(("parallel",)),
        name="matmul_residual",
    )(a, w, x)


def _ffn_kernel(x_ref, wn_ref, wa_ref, wb_ref, wd_ref, o_ref, h_ref):
    j = pl.program_id(1)

    @pl.when(j == 0)
    def _():
        h_ref[...] = _rms(x_ref[...], wn_ref[...]).astype(BF16)

    h = h_ref[...]
    a = jnp.dot(h, wa_ref[...], preferred_element_type=F32)
    b = jnp.dot(h, wb_ref[...], preferred_element_type=F32)
    act = (a * jax.nn.sigmoid(a) * b).astype(BF16)
    y = jnp.dot(act, wd_ref[...], preferred_element_type=F32)

    @pl.when(j == 0)
    def _():
        o_ref[...] = x_ref[...] + y

    @pl.when(j > 0)
    def _():
        o_ref[...] += y


def dense_ffn(x, wn, w_up, w_down):
    rows = x.shape[0]
    nf = DFF // FFN_FC
    return pl.pallas_call(
        _ffn_kernel,
        grid=(rows // TM, nf),
        in_specs=[pl.BlockSpec((TM, D), lambda i, j: (i, 0)),
                  pl.BlockSpec((1, D), lambda i, j: (0, 0)),
                  pl.BlockSpec((D, FFN_FC), lambda i, j: (0, j)),
                  pl.BlockSpec((D, FFN_FC), lambda i, j: (0, nf + j)),
                  pl.BlockSpec((FFN_FC, D), lambda i, j: (j, 0))],
        out_specs=pl.BlockSpec((TM, D), lambda i, j: (i, 0)),
        out_shape=jax.ShapeDtypeStruct((rows, D), F32),
        scratch_shapes=[pltpu.VMEM((TM, D), BF16)],
        compiler_params=_cparams(("parallel", "arbitrary")),
        name="dense_ffn",
    )(x, wn.reshape(1, D), w_up, w_up, w_down)


def _hgrn_masks(C):
    nlev = int(math.log2(C))
    t = np.arange(C)[:, None]
    s = np.arange(C)[None, :]
    m = [(t == s)]
    for lev in range(nlev):
        b = 1 << lev
        m.append((t // (2 * b) == s // (2 * b)) & ((t // b) % 2 == 1) & ((s // b) % 2 == 0))
    return np.stack(m).astype(np.float32)


def _hgrn_kernel(proj_ref, lb_ref, nw_ref, msk_ref, s0_ref, y_ref, s_ref, *, C, n_valid):
    nlev = int(math.log2(C))
    c = pl.program_id(1)

    @pl.when(c == 0)
    def _():
        s_ref[...] = s0_ref[...]

    rloc = lax.broadcasted_iota(jnp.int32, (C, HD), 0)
    valid = (c * C + rloc) < n_valid
    nt = (((1,), (1,)), ((), ()))
    tn = (((0,), (0,)), ((), ()))
    o_heads = []
    for h in range(H):
        col = slice(h * HD, (h + 1) * HD)
        lb = lb_ref[:, col]
        q = proj_ref[:, h * HD:(h + 1) * HD]
        f = proj_ref[:, D + h * HD:D + (h + 1) * HD]
        v = proj_ref[:, 2 * D + h * HD:2 * D + (h + 1) * HD]
        qs = q * jax.nn.sigmoid(q)
        fg = lb + (1.0 - lb) * jax.nn.sigmoid(f)
        k = jnp.where(valid, 1.0 - fg, 0.0)
        g = jnp.where(valid, jnp.log(fg), 0.0)
        G = g
        sh = 1
        while sh < C:
            G = G + jnp.where(rloc >= sh, pltpu.roll(G, sh, 0), 0.0)
            sh *= 2
        vb = v.astype(BF16)
        A = msk_ref[0] * lax.dot_general(qs.astype(BF16), k.astype(BF16), nt,
                                         preferred_element_type=F32)
        for lev in range(nlev):
            b = 1 << lev
            if b >= 8:
                gm = G.reshape(C // (2 * b), 2 * b, HD)[:, b - 1:b, :]
                gmid = jnp.broadcast_to(gm, (C // (2 * b), 2 * b, HD)).reshape(C, HD)
            else:
                g8 = G.reshape(C // 8, 8, HD)
                sub = lax.broadcasted_iota(jnp.int32, (C // 8, 8, HD), 1)
                gmid = None
                for base in range(0, 8, 2 * b):
                    row = jnp.broadcast_to(g8[:, base + b - 1:base + b, :], (C // 8, 8, HD))
                    gmid = row if gmid is None else jnp.where(sub >= base, row, gmid)
                gmid = gmid.reshape(C, HD)
            right = (rloc // b) % 2 == 1
            fac = jnp.exp(jnp.where(right, G - gmid, gmid - G))
            qt = jnp.where(right, qs * fac, 0.0).astype(BF16)
            kt = jnp.where(right, 0.0, k * fac).astype(BF16)
            A = A + msk_ref[lev + 1] * lax.dot_general(qt, kt, nt, preferred_element_type=F32)
        g_end = G[C - 1:C, :]
        qh = (qs * jnp.exp(G)).astype(BF16)
        kh = (k * jnp.exp(g_end - G)).astype(BF16)
        S = s_ref[0, h]
        o_h = (jnp.dot(A.astype(BF16), vb, preferred_element_type=F32)
               + jnp.dot(qh, S.astype(BF16), preferred_element_type=F32))
        e_col = jnp.broadcast_to(jnp.exp(g_end), (HD, HD)).T
        s_ref[0, h] = e_col * S + lax.dot_general(kh, vb, tn, preferred_element_type=F32)
        o_heads.append(o_h)
    o = jnp.concatenate(o_heads, axis=1)
    gate = proj_ref[:, 3 * D:4 * D]
    y_ref[...] = _rms(o * jax.nn.sigmoid(gate), nw_ref[...]).astype(BF16)


def hgrn_mixer(proj, lb, nw, s0, nb, C, n_chunks, n_valid, s0_per_batch):
    msk = jnp.asarray(_hgrn_masks(C))
    nlev1 = msk.shape[0]
    s0_map = (lambda b, c: (b, 0, 0, 0)) if s0_per_batch else (lambda b, c: (0, 0, 0, 0))
    return pl.pallas_call(
        functools.partial(_hgrn_kernel, C=C, n_valid=n_valid),
        grid=(nb, n_chunks),
        in_specs=[pl.BlockSpec((C, 4 * D), lambda b, c: (b * n_chunks + c, 0)),
                  pl.BlockSpec((1, D), lambda b, c: (0, 0)),
                  pl.BlockSpec((1, D), lambda b, c: (0, 0)),
                  pl.BlockSpec((nlev1, C, C), lambda b, c: (0, 0, 0)),
                  pl.BlockSpec((1, H, HD, HD), s0_map)],
        out_specs=[pl.BlockSpec((C, D), lambda b, c: (b * n_chunks + c, 0)),
                   pl.BlockSpec((1, H, HD, HD), lambda b, c: (b, 0, 0, 0))],
        out_shape=[jax.ShapeDtypeStruct((nb * n_chunks * C, D), BF16),
                   jax.ShapeDtypeStruct((nb, H, HD, HD), F32)],
        compiler_params=_cparams(("parallel", "arbitrary")),
        name="hgrn_mixer",
    )(proj, lb.reshape(1, D), nw.reshape(1, D), msk, s0)


def _bucket_of_distance(dist):
    n = np.maximum(np.asarray(dist), 0)
    max_exact = NBUCKETS // 2
    nf = np.maximum(n, 1).astype(np.float64)
    large = max_exact + (np.log(nf / max_exact) / math.log(MAXDIST / max_exact)
                         * (NBUCKETS - max_exact)).astype(np.int64)
    return np.where(n < max_exact, n, np.minimum(large, NBUCKETS - 1))


def _bias_by_distance(dist, table):
    dist = np.asarray(dist)
    onehot = np.eye(NBUCKETS, dtype=np.float32)[_bucket_of_distance(dist).reshape(-1)]
    out = jnp.einsum("nb,bh->hn", jnp.asarray(onehot), table.astype(F32),
                     precision=lax.Precision.HIGHEST)
    return out.reshape((H,) + dist.shape)


def _toeplitz(fvals, T):
    w = jnp.pad(fvals[:, ::-1], ((0, 0), (0, 1)))
    flat = jnp.tile(w, (1, T))[:, :T * (2 * T - 1)]
    return flat.reshape(H, T, 2 * T - 1)[:, :, T - 1:]


def _subln(d, w):
    return _rms(d, w) * (1.0 - LAMBDA_INIT)


def _attn_prompt_kernel(lam_ref, q_ref, k_ref, v_ref, strip_ref, sw_ref, o_ref, s_scr):
    T = AT_T
    i = pl.program_id(2)
    nt = (((1,), (1,)), ((), ()))
    qb = q_ref[0, 0]
    lane = lax.broadcasted_iota(jnp.int32, (T, HD), 1)
    zero = jnp.zeros_like(qb)
    qq = jnp.concatenate([jnp.where(lane < DQK, qb, zero), jnp.where(lane >= DQK, qb, zero)], axis=0)

    def score_tile(c, mx):
        kc = k_ref[0, 0, pl.ds(pl.multiple_of(c * T, T), T), :]
        bt = strip_ref[0, jnp.where(c > i, 3, jnp.minimum(i - c, 2))]
        s = lax.dot_general(qq, kc, nt, preferred_element_type=F32) + jnp.concatenate([bt, bt], axis=0)
        s_scr[c] = s
        return jnp.maximum(mx, jnp.maximum(s[:, :HD], s[:, HD:]))

    n_pairs = (i + 2) // 2
    mx = lax.fori_loop(0, n_pairs, lambda jp, mx: score_tile(2 * jp + 1, score_tile(2 * jp, mx)),
                       jnp.full((2 * T, HD), NEG, F32))
    m = jnp.max(mx, axis=-1, keepdims=True)

    def pv_tile(c, carry):
        acc, ls = carry
        p = jnp.exp(s_scr[c] - m)
        vc = v_ref[0, 0, pl.ds(pl.multiple_of(c * T, T), T), :]
        return (acc + jnp.dot(p.astype(BF16), vc, preferred_element_type=F32),
                ls + (p[:, :HD] + p[:, HD:]))

    acc, ls = lax.fori_loop(0, n_pairs, lambda jp, cr: pv_tile(2 * jp + 1, pv_tile(2 * jp, cr)),
                            (jnp.zeros((2 * T, HD), F32), jnp.zeros((2 * T, HD), F32)))
    o = acc / jnp.sum(ls, axis=-1, keepdims=True)
    o_ref[...] = _subln(o[:T] - lam_ref[0] * o[T:], sw_ref[...])


def attn_prompt(q, k, v, strips, lam, sw):
    T = AT_T
    nq = LA // T
    smem = pl.BlockSpec(memory_space=pltpu.SMEM)
    return pl.pallas_call(
        _attn_prompt_kernel,
        grid=(B, H, nq),
        in_specs=[smem,
                  pl.BlockSpec((1, 1, T, HD), lambda b, h, i: (b, h, i, 0)),
                  pl.BlockSpec((1, 1, LA + T, HD), lambda b, h, i: (b, h, 0, 0)),
                  pl.BlockSpec((1, 1, LA + T, HD), lambda b, h, i: (b, h, 0, 0)),
                  pl.BlockSpec((1, 4, T, T), lambda b, h, i: (h, 0, 0, 0)),
                  pl.BlockSpec((1, HD), lambda b, h, i: (0, 0))],
        out_specs=pl.BlockSpec((T, HD), lambda b, h, i: (b * nq + i, h)),
        out_shape=jax.ShapeDtypeStruct((B * LA, D), F32),
        scratch_shapes=[pltpu.VMEM((nq + 1, 2 * T, T), F32)],
        compiler_params=_cparams(("parallel", "parallel", "arbitrary")),
        name="attn_prompt",
    )(lam, q, k, v, strips, sw.reshape(1, HD))


def _attn_decode_kernel(pt_ref, lam_ref, q_ref, *refs):
    del pt_ref
    k_refs = refs[:PP]
    v_refs = refs[PP:2 * PP]
    bias_ref, kn_ref, vn_ref, bn_ref, sw_ref, o_ref, m_ref, l_ref, acc_ref = refs[2 * PP:]
    g = pl.program_id(1)
    nt = (((1,), (1,)), ((), ()))

    @pl.when(g == 0)
    def _():
        m_ref[...] = jnp.full(m_ref.shape, NEG, F32)
        l_ref[...] = jnp.zeros(l_ref.shape, F32)
        acc_ref[...] = jnp.zeros(acc_ref.shape, F32)

    qrows = q_ref[0]

    def softmax_update(s):
        m_old = m_ref[...]
        m_new = jnp.maximum(m_old, jnp.max(s, axis=-1, keepdims=True))
        alpha = jnp.exp(m_old - m_new)
        p = jnp.exp(s - m_new)
        l_ref[...] = alpha * l_ref[...] + jnp.sum(p, axis=-1, keepdims=True)
        m_ref[...] = m_new
        return p.astype(BF16), alpha

    def accumulate(p, alpha, v_of_head):
        for hh in range(H):
            rows = slice(hh * 16, (hh + 1) * 16)
            acc_ref[rows, :] = alpha[rows] * acc_ref[rows, :] + jnp.dot(
                p[rows], v_of_head(hh).astype(BF16), preferred_element_type=F32)

    for j in range(PP):
        s = jnp.dot(qrows, k_refs[j][...].astype(BF16), preferred_element_type=F32) + bias_ref[j]
        p, alpha = softmax_update(s)
        accumulate(p, alpha, lambda hh, j=j: v_refs[j][pl.ds(hh, PAGE, stride=H), :])

    @pl.when(g == pl.num_programs(1) - 1)
    def _():
        s = lax.dot_general(qrows, kn_ref[0].astype(BF16), nt,
                            preferred_element_type=F32) + bn_ref[...]
        p, alpha = softmax_update(s)
        accumulate(p, alpha, lambda hh: vn_ref[0, :, hh * HD:(hh + 1) * HD])
        o = acc_ref[...] / l_ref[...]
        outs = []
        for hh in range(H):
            o1 = o[hh * 16:hh * 16 + 8]
            o2 = o[hh * 16 + 8:hh * 16 + 16]
            outs.append(_subln(o1 - lam_ref[0] * o2, sw_ref[...]))
        o_ref[0] = jnp.concatenate(outs, axis=1)


def attn_decode(page_table, lam, qrows, cache_kt, cache_v2, bias_pages, k_new, v_new, bias_new, sw):
    n_steps = NPAGES // PP

    def page_spec(j):
        return pl.BlockSpec((D, PAGE), lambda b, g, pt: (pt[b * NPAGES + g * PP + j], 0))

    smem = pl.BlockSpec(memory_space=pltpu.SMEM)
    grid_spec = pltpu.PrefetchScalarGridSpec(
        num_scalar_prefetch=1,
        grid=(DB, n_steps),
        in_specs=([smem, pl.BlockSpec((1, HD, D), lambda b, g, pt: (b, 0, 0))]
                  + [page_spec(j) for j in range(PP)]
                  + [page_spec(j) for j in range(PP)]
                  + [pl.BlockSpec((PP, HD, PAGE), lambda b, g, pt: (g, 0, 0)),
                     pl.BlockSpec((1, 16, D), lambda b, g, pt: (b, 0, 0)),
                     pl.BlockSpec((1, 16, D), lambda b, g, pt: (b, 0, 0)),
                     pl.BlockSpec((HD, 16), lambda b, g, pt: (0, 0)),
                     pl.BlockSpec((1, HD), lambda b, g, pt: (0, 0))]),
        out_specs=pl.BlockSpec((1, 8, D), lambda b, g, pt: (b, 0, 0)),
        scratch_shapes=[pltpu.VMEM((HD, 1), F32), pltpu.VMEM((HD, 1), F32),
                        pltpu.VMEM((HD, HD), F32)],
    )
    return pl.pallas_call(
        _attn_decode_kernel,
        grid_spec=grid_spec,
        out_shape=jax.ShapeDtypeStruct((DB, 8, D), F32),
        compiler_params=_cparams(("parallel", "arbitrary")),
        name="attn_decode",
    )(page_table.reshape(-1), lam, qrows, *([cache_kt] * PP), *([cache_v2] * PP),
      bias_pages, k_new, v_new, bias_new, sw.reshape(1, HD))


def _router_kernel(x_ref, wn_ref, wr_ref, br_ref, h_ref, idx_ref, gate_ref):
    h = _rms(x_ref[...], wn_ref[...])
    h_ref[...] = h
    h_hi = h.astype(BF16)
    h_lo = (h - h_hi.astype(F32)).astype(BF16)
    logits = (jnp.dot(h_hi, wr_ref[0], preferred_element_type=F32)
              + jnp.dot(h_lo, wr_ref[0], preferred_element_type=F32)
              + jnp.dot(h_hi, wr_ref[1], preferred_element_type=F32)) + br_ref[...]
    lane = lax.broadcasted_iota(jnp.int32, logits.shape, 1)
    ninf = jnp.float32(-jnp.inf)
    logits = jnp.where(lane < NEXP, logits, ninf)
    v1 = jnp.max(logits, axis=-1, keepdims=True)
    i1 = jnp.min(jnp.where(logits == v1, lane, HD), axis=-1, keepdims=True)
    rest = jnp.where(lane == i1, ninf, logits)
    v2 = jnp.max(rest, axis=-1, keepdims=True)
    i2 = jnp.min(jnp.where(rest == v2, lane, HD), axis=-1, keepdims=True)
    e = jnp.exp(v2 - v1)
    g1 = 1.0 / (1.0 + e)
    g2 = e / (1.0 + e)
    idx_ref[...] = jnp.where(lane == 0, i1, jnp.where(lane == 1, i2, 0))
    gate_ref[...] = jnp.where(lane == 0, g1, jnp.where(lane == 1, g2, 0.0))


def moe_router(x, wn, w_router, b_router):
    rows = x.shape[0]
    wr = jnp.zeros((D, HD), F32).at[:, :NEXP].set(w_router.astype(F32))
    wr_hi = wr.astype(BF16)
    wr_lo = (wr - wr_hi.astype(F32)).astype(BF16)
    br = jnp.zeros((1, HD), F32).at[0, :NEXP].set(b_router.astype(F32))
    return pl.pallas_call(
        _router_kernel,
        grid=(rows // TM,),
        in_specs=[pl.BlockSpec((TM, D), lambda i: (i, 0)),
                  pl.BlockSpec((1, D), lambda i: (0, 0)),
                  pl.BlockSpec((2, D, HD), lambda i: (0, 0, 0)),
                  pl.BlockSpec((1, HD), lambda i: (0, 0))],
        out_specs=[pl.BlockSpec((TM, D), lambda i: (i, 0)),
                   pl.BlockSpec((TM, HD), lambda i: (i, 0)),
                   pl.BlockSpec((TM, HD), lambda i: (i, 0))],
        out_shape=[jax.ShapeDtypeStruct((rows, D), F32),
                   jax.ShapeDtypeStruct((rows, HD), jnp.int32),
                   jax.ShapeDtypeStruct((rows, HD), F32)],
        compiler_params=_cparams(("parallel",)),
        name="moe_router",
    )(x, wn.reshape(1, D), jnp.stack([wr_hi, wr_lo]), br)


def _moe_kernel(te_ref, nv_ref, sa_ref, h_hbm, wa_ref, wb_ref, wd_ref, out_hbm,
                hbuf, hb16, obuf, gsem, ssem):
    del te_ref
    i = pl.program_id(0)
    j = pl.program_id(1)
    nt = pl.num_programs(0)
    last_j = pl.num_programs(1) - 1
    slot = lax.rem(i, 2)
    live = nv_ref[i] > 0

    def gather(tile, s):
        def copy(r, a):
            return pltpu.make_async_copy(h_hbm.at[pl.ds(lax.shift_right_logical(a, 1), 1)],
                                         hbuf.at[s, pl.ds(r, 1)], gsem.at[s])

        def start(r, carry):
            copy(r, sa_ref[tile * MOE_TM + r]).start()
            return carry

        def wait(r, carry):
            copy(r, 0).wait()
            return carry

        return start, wait

    def scatter(tile, s):
        def copy(r, a):
            return pltpu.make_async_copy(obuf.at[s, pl.ds(r, 1)], out_hbm.at[pl.ds(a, 1)], ssem.at[s])

        def start(r, carry):
            copy(r, sa_ref[tile * MOE_TM + r]).start()
            return carry

        def wait(r, carry):
            copy(r, 0).wait()
            return carry

        return start, wait

    @pl.when(j == 0)
    def _():
        prev = jnp.maximum(i - 2, 0)

        @pl.when(i >= 2)
        def _():
            lax.fori_loop(0, nv_ref[prev], scatter(prev, slot)[1], 0)

        @pl.when(live)
        def _():
            @pl.when(i == 0)
            def _():
                lax.fori_loop(0, MOE_TM, gather(0, 0)[0], 0)

            nxt = jnp.minimum(i + 1, nt - 1)

            @pl.when((i + 1 < nt) & (nv_ref[nxt] > 0))
            def _():
                lax.fori_loop(0, MOE_TM, gather(nxt, 1 - slot)[0], 0)

            lax.fori_loop(0, MOE_TM, gather(i, slot)[1], 0)
            hb16[...] = hbuf[slot].astype(BF16)

    @pl.when(live)
    def _():
        h = hb16[...]
        a = jnp.dot(h, wa_ref[0], preferred_element_type=F32)
        b = jnp.dot(h, wb_ref[0], preferred_element_type=F32)
        act = (a * jax.nn.sigmoid(a) * b).astype(BF16)
        y = jnp.dot(act, wd_ref[0], preferred_element_type=F32)

        @pl.when(j == 0)
        def _():
            obuf[slot] = y

        @pl.when(j > 0)
        def _():
            obuf[slot] += y

        @pl.when(j == last_j)
        def _():
            lax.fori_loop(0, nv_ref[i], scatter(i, slot)[0], 0)

    @pl.when((i == nt - 1) & (j == last_j))
    def _():
        lax.fori_loop(0, nv_ref[nt - 2], scatter(nt - 2, 1 - slot)[1], 0)
        lax.fori_loop(0, nv_ref[nt - 1], scatter(nt - 1, slot)[1], 0)


def moe_experts(tile_expert, tile_rows, slot_assign, h, w_up, w_down):
    nf = DFFE // MOE_FC
    grid_spec = pltpu.PrefetchScalarGridSpec(
        num_scalar_prefetch=3,
        grid=(MOE_NT, nf),
        in_specs=[pl.BlockSpec(memory_space=pl.ANY),
                  pl.BlockSpec((1, D, MOE_FC), lambda i, j, te, nv, sa: (te[i], 0, j)),
                  pl.BlockSpec((1, D, MOE_FC), lambda i, j, te, nv, sa: (te[i], 0, nf + j)),
                  pl.BlockSpec((1, MOE_FC, D), lambda i, j, te, nv, sa: (te[i], j, 0))],
        out_specs=pl.BlockSpec(memory_space=pl.ANY),
        scratch_shapes=[pltpu.VMEM((2, MOE_TM, D), F32), pltpu.VMEM((MOE_TM, D), BF16),
                        pltpu.VMEM((2, MOE_TM, D), F32),
                        pltpu.SemaphoreType.DMA((2,)), pltpu.SemaphoreType.DMA((2,))],
    )
    return pl.pallas_call(
        _moe_kernel,
        grid_spec=grid_spec,
        out_shape=jax.ShapeDtypeStruct((2 * R, D), F32),
        compiler_params=_cparams(("arbitrary", "arbitrary")),
        name="moe_experts",
    )(tile_expert, tile_rows, slot_assign, h, w_up, w_up, w_down)


def _combine_kernel(x_ref, g_ref, e_ref, wn_ref, y_ref):
    moe = g_ref[:, 0:1] * e_ref[:, :D] + g_ref[:, 1:2] * e_ref[:, D:]
    y_ref[...] = _rms(x_ref[...] + moe, wn_ref[...])


def moe_combine_norm(x, gates, eo2, wn):
    rows = x.shape[0]
    return pl.pallas_call(
        _combine_kernel,
        grid=(rows // TM,),
        in_specs=[pl.BlockSpec((TM, D), lambda i: (i, 0)),
                  pl.BlockSpec((TM, HD), lambda i: (i, 0)),
                  pl.BlockSpec((TM, 2 * D), lambda i: (i, 0)),
                  pl.BlockSpec((1, D), lambda i: (0, 0))],
        out_specs=pl.BlockSpec((TM, D), lambda i: (i, 0)),
        out_shape=jax.ShapeDtypeStruct((rows, D), F32),
        compiler_params=_cparams(("parallel",)),
        name="moe_combine_norm",
    )(x, gates, eo2, wn.reshape(1, D))


def _moe_plan(idx):
    e_flat = idx.reshape(-1)
    onehot = (e_flat[:, None] == jnp.arange(NEXP, dtype=jnp.int32)[None, :]).astype(jnp.int32)
    cum = jnp.cumsum(onehot, axis=0)
    counts = cum[-1]
    padded = ((counts + MOE_TM - 1) // MOE_TM) * MOE_TM
    ends = jnp.cumsum(padded)
    starts = ends - padded
    slot = jnp.sum(onehot * (starts[None, :] + cum - onehot), axis=1)
    slot_assign = jnp.zeros((MOE_NT * MOE_TM,), jnp.int32).at[slot].set(
        jnp.arange(2 * R, dtype=jnp.int32))
    tile_start = jnp.arange(MOE_NT, dtype=jnp.int32) * MOE_TM
    owner = (tile_start[:, None] >= starts[None, :]) & (tile_start[:, None] < ends[None, :])
    tile_rows = jnp.sum(jnp.where(owner, jnp.clip(counts[None, :] - (tile_start[:, None] - starts[None, :]),
                                                  0, MOE_TM), 0), axis=1).astype(jnp.int32)
    last_expert = jnp.max(jnp.where(counts > 0, jnp.arange(NEXP, dtype=jnp.int32), 0))
    tile_expert = jnp.where(jnp.any(owner, axis=1), jnp.argmax(owner, axis=1), last_expert).astype(jnp.int32)
    return tile_expert, tile_rows, slot_assign


def kernel(x_prompt, x_sample, state_hgrn, cache_k, cache_v, page_table, meta_tokens,
           norm_mix_w, norm_ffn_w, hg_w_in, hg_lower_bound, hg_norm_w, hg_w_out,
           at_w_qkv, at_lambda_q1, at_lambda_k1, at_lambda_q2, at_lambda_k2, at_subln_w,
           at_w_out, rel_bias_table, ff_w_up, ff_w_down, moe_w_router, moe_b_router,
           moe_w_up, moe_w_down, final_norm_w):
    xp = jnp.concatenate([jnp.broadcast_to(meta_tokens.astype(F32), (B, NMETA, D)), x_prompt,
                          jnp.zeros((B, LP - LREAL, D), F32)], axis=1)
    x0 = jnp.concatenate([xp.reshape(RP, D), x_sample.reshape(RS, D)], axis=0)
    lower = jnp.cumsum(jax.nn.softmax(hg_lower_bound.astype(F32), axis=0), axis=0)

    proj = norm_matmul(x0, norm_mix_w[0], hg_w_in[0].astype(BF16), 1024)
    y_p, st_p = hgrn_mixer(proj, lower[0], hg_norm_w[0], jnp.zeros((1, H, HD, HD), F32),
                           B, HG_C, LP // HG_C, LREAL, False)
    proj_s = jnp.pad(proj[RP:].reshape(DB, DT, 4 * D), ((0, 0), (0, HG_CS - DT), (0, 0)))
    y_s, st_s = hgrn_mixer(proj_s.reshape(DB * HG_CS, 4 * D), lower[0], hg_norm_w[0], state_hgrn[0],
                           DB, HG_CS, 1, DT, True)
    y0 = jnp.concatenate([y_p, y_s.reshape(DB, HG_CS, D)[:, :DT].reshape(RS, D)], axis=0)
    x1 = matmul_residual(y0, hg_w_out[0].astype(BF16), x0)
    x2 = dense_ffn(x1, norm_ffn_w[0], ff_w_up[0].astype(BF16), ff_w_down[0].astype(BF16))

    f = F32
    lam = (jnp.exp(jnp.sum(at_lambda_q1[0].astype(f) * at_lambda_k1[0].astype(f)))
           - jnp.exp(jnp.sum(at_lambda_q2[0].astype(f) * at_lambda_k2[0].astype(f))) + LAMBDA_INIT)
    lam = lam.reshape(1).astype(F32)
    qkv = norm_matmul(x2, norm_mix_w[1], at_w_qkv[0].astype(BF16), 1024)
    q_all, k_all, v_all = qkv[:, :D] * (DQK ** -0.5), qkv[:, D:2 * D], qkv[:, 2 * D:]

    def heads_major(t, length):
        t = t[:RP].reshape(B, LP, H, HD).transpose(0, 2, 1, 3).astype(BF16)
        return jnp.pad(t, ((0, 0), (0, 0), (0, length - LP), (0, 0)))

    T = AT_T
    dvals = np.arange(-(T - 1), T)
    f_diag = jnp.where(jnp.asarray(dvals >= 0)[None, :], _bias_by_distance(dvals, rel_bias_table), NEG)
    f_prev = _bias_by_distance(dvals + T, rel_bias_table)
    cfar = rel_bias_table[NBUCKETS - 1].astype(F32)
    strips = jnp.stack([_toeplitz(f_diag, T), _toeplitz(f_prev, T),
                        jnp.broadcast_to(cfar[:, None, None], (H, T, T)),
                        jnp.full((H, T, T), NEG, F32)], axis=1)
    o_p = attn_prompt(heads_major(q_all, LA), heads_major(k_all, LA + T), heads_major(v_all, LA + T),
                      strips, lam, at_subln_w[0])

    q_s = q_all[RP:].reshape(DB, DT, H, 2, DQK)
    sel = jnp.einsum("hj,mn->hmjn", jnp.eye(H, dtype=F32), jnp.eye(2, dtype=F32))
    qrows = jnp.einsum("bthmd,hmjn->bhmtjnd", q_s, sel)
    qrows = jnp.pad(qrows, ((0, 0),) * 3 + ((0, 8 - DT),) + ((0, 0),) * 3)
    qrows = qrows.reshape(DB, HD, D).astype(BF16)
    tq = np.minimum(np.arange(8), DT - 1)
    kpos = np.arange(PAST).reshape(NPAGES, 1, PAGE)
    bp = _bias_by_distance(PAST + tq[None, :, None] - kpos, rel_bias_table)
    bias_pages = jnp.broadcast_to(bp.transpose(1, 0, 2, 3)[:, :, None], (NPAGES, H, 2, 8, PAGE))
    bias_pages = bias_pages.reshape(NPAGES, HD, PAGE)
    sn = np.arange(16)
    dn = tq[:, None] - sn[None, :]
    bn = jnp.where(jnp.asarray((dn >= 0) & (sn[None, :] < DT))[None],
                   _bias_by_distance(dn, rel_bias_table), NEG)
    bias_new = jnp.broadcast_to(bn[:, None], (H, 2, 8, 16)).reshape(HD, 16)
    k_new = jnp.pad(k_all[RP:].reshape(DB, DT, D), ((0, 0), (0, 16 - DT), (0, 0)))
    v_new = jnp.pad(v_all[RP:].reshape(DB, DT, D), ((0, 0), (0, 16 - DT), (0, 0)))
    n_phys = cache_k.shape[1]
    cache_kt = jnp.transpose(cache_k[0], (0, 2, 3, 4, 1)).reshape(n_phys * D, PAGE)
    cache_v2 = cache_v[0].reshape(n_phys * PAGE * H, HD)
    o_s = attn_decode(page_table, lam, qrows, cache_kt, cache_v2, bias_pages, k_new, v_new, bias_new,
                      at_subln_w[0])
    o1 = jnp.concatenate([o_p.reshape(B, LA, D)[:, :LP].reshape(RP, D),
                          o_s[:, :DT].reshape(RS, D)], axis=0)
    x3 = matmul_residual(o1, at_w_out[0].astype(BF16), x2)

    h_moe, idx, gates = moe_router(x3, norm_ffn_w[1], moe_w_router[0], moe_b_router[0])
    tile_expert, tile_rows, slot_assign = _moe_plan(idx[:, :TOPK])
    eo = moe_experts(tile_expert, tile_rows, slot_assign, h_moe,
                     moe_w_up[0].astype(BF16), moe_w_down[0].astype(BF16))
    y = moe_combine_norm(x3, gates, eo.reshape(R, 2 * D), final_norm_w)

    y_prompt = y[:RP].reshape(B, LP, D)[:, NMETA:LREAL]
    y_sample = y[RP:].reshape(DB, DT, D)
    k_p = k_all[:RP].reshape(B, LP, H, 2, DQK)[:, :LREAL]
    v_p = v_all[:RP].reshape(B, LP, H, HD)[:, :LREAL]
    k_s = k_all[RP:].reshape(DB, DT, H, 2, DQK)
    v_s = v_all[RP:].reshape(DB, DT, H, HD)
    return (y_prompt, y_sample, st_p[None], st_s[None], k_p[None], v_p[None], k_s[None], v_s[None])
```

```python
import functools
import math

import numpy as np
import jax
import jax.numpy as jnp
from jax import lax
from jax.experimental import pallas as pl
from jax.experimental.pallas import tpu as pltpu

F32 = jnp.float32
BF16 = jnp.bfloat16

D = 1024
B = 8
SEQ = 2048
DB = 128
DT = 4
PAST = 2048
PAGE = 128
NPAGES = PAST // PAGE
NMETA = 16
LREAL = NMETA + SEQ
LP = 2176
RP = B * LP
RS = DB * DT
R = RP + RS
H = 8
HD = 128
DQK = 64
NBUCKETS = 32
MAXDIST = 128
DFF = 2816
NEXP = 8
TOPK = 2
DFFE = 3584
EPS = 1e-6
NEG = -1e30
LAMBDA_INIT = 0.8 - 0.6 * math.exp(-0.3 * 1)

VMEM_LIMIT = 56 * 1024 * 1024

TM = 512
HG_C = 128
HG_CS = 16
AT_T = 256
LA = 2304
MOE_TM = 512
MOE_NT = (2 * R) // MOE_TM + NEXP
MOE_FC = 1792
FFN_FC = 1408
PP = 4


def _cparams(sem, vmem=VMEM_LIMIT):
    return pltpu.CompilerParams(dimension_semantics=sem, vmem_limit_bytes=vmem)


def _rms(x, w):
    return (x * lax.rsqrt(jnp.mean(x * x, axis=-1, keepdims=True) + EPS)) * w


def _norm_matmul_kernel(x_ref, wn_ref, w_ref, o_ref, h_ref):
    @pl.when(pl.program_id(1) == 0)
    def _():
        h_ref[...] = _rms(x_ref[...], wn_ref[...]).astype(BF16)

    o_ref[...] = jnp.dot(h_ref[...], w_ref[...], preferred_element_type=F32)


def norm_matmul(x, wn, w, tn):
    rows, n = x.shape[0], w.shape[1]
    return pl.pallas_call(
        _norm_matmul_kernel,
        grid=(rows // TM, n // tn),
        in_specs=[pl.BlockSpec((TM, D), lambda i, j: (i, 0)),
                  pl.BlockSpec((1, D), lambda i, j: (0, 0)),
                  pl.BlockSpec((D, tn), lambda i, j: (0, j))],
        out_specs=pl.BlockSpec((TM, tn), lambda i, j: (i, j)),
        out_shape=jax.ShapeDtypeStruct((rows, n), F32),
        scratch_shapes=[pltpu.VMEM((TM, D), BF16)],
        compiler_params=_cparams(("parallel", "arbitrary")),
        name="norm_matmul",
    )(x, wn.reshape(1, D), w)


def _matmul_residual_kernel(a_ref, w_ref, x_ref, o_ref):
    o_ref[...] = x_ref[...] + jnp.dot(a_ref[...].astype(BF16), w_ref[...],
                                      preferred_element_type=F32)


def matmul_residual(a, w, x):
    rows = x.shape[0]
    return pl.pallas_call(
        _matmul_residual_kernel,
        grid=(rows // TM,),
        in_specs=[pl.BlockSpec((TM, D), lambda i: (i, 0)),
                  pl.BlockSpec((D, D), lambda i: (0, 0)),
                  pl.BlockSpec((TM, D), lambda i: (i, 0))],
        out_specs=pl.BlockSpec((TM, D), lambda i: (i, 0)),
        out_shape=jax.ShapeDtypeStruct((rows, D), F32),
        compiler_params=_cparams(("parallel",)),
        name="matmul_residual",
    )(a, w, x)


def _ffn_kernel(x_ref, wn_ref, wa_ref, wb_ref, wd_ref, o_ref, h_ref):
    j = pl.program_id(1)

    @pl.when(j == 0)
    def _():
        h_ref[...] = _rms(x_ref[...], wn_ref[...]).astype(BF16)

    h = h_ref[...]
    a = jnp.dot(h, wa_ref[...], preferred_element_type=F32)
    b = jnp.dot(h, wb_ref[...], preferred_element_type=F32)
    act = (a * jax.nn.sigmoid(a) * b).astype(BF16)
    y = jnp.dot(act, wd_ref[...], preferred_element_type=F32)

    @pl.when(j == 0)
    def _():
        o_ref[...] = x_ref[...] + y

    @pl.when(j > 0)
    def _():
        o_ref[...] += y


def dense_ffn(x, wn, w_up, w_down):
    rows = x.shape[0]
    nf = DFF // FFN_FC
    return pl.pallas_call(
        _ffn_kernel,
        grid=(rows // TM, nf),
        in_specs=[pl.BlockSpec((TM, D), lambda i, j: (i, 0)),
                  pl.BlockSpec((1, D), lambda i, j: (0, 0)),
                  pl.BlockSpec((D, FFN_FC), lambda i, j: (0, j)),
                  pl.BlockSpec((D, FFN_FC), lambda i, j: (0, nf + j)),
                  pl.BlockSpec((FFN_FC, D), lambda i, j: (j, 0))],
        out_specs=pl.BlockSpec((TM, D), lambda i, j: (i, 0)),
        out_shape=jax.ShapeDtypeStruct((rows, D), F32),
        scratch_shapes=[pltpu.VMEM((TM, D), BF16)],
        compiler_params=_cparams(("parallel", "arbitrary")),
        name="dense_ffn",
    )(x, wn.reshape(1, D), w_up, w_up, w_down)


def _hgrn_masks(C):
    nlev = int(math.log2(C))
    t = np.arange(C)[:, None]
    s = np.arange(C)[None, :]
    m = [(t == s)]
    for lev in range(nlev):
        b = 1 << lev
        m.append((t // (2 * b) == s // (2 * b)) & ((t // b) % 2 == 1) & ((s // b) % 2 == 0))
    return np.stack(m).astype(np.float32)


def _hgrn_kernel(proj_ref, lb_ref, nw_ref, msk_ref, s0_ref, y_ref, s_ref, *, C, n_valid):
    nlev = int(math.log2(C))
    c = pl.program_id(1)

    @pl.when(c == 0)
    def _():
        s_ref[...] = s0_ref[...]

    rloc = lax.broadcasted_iota(jnp.int32, (C, HD), 0)
    valid = (c * C + rloc) < n_valid
    nt = (((1,), (1,)), ((), ()))
    tn = (((0,), (0,)), ((), ()))
    o_heads = []
    for h in range(H):
        col = slice(h * HD, (h + 1) * HD)
        lb = lb_ref[:, col]
        q = proj_ref[:, h * HD:(h + 1) * HD]
        f = proj_ref[:, D + h * HD:D + (h + 1) * HD]
        v = proj_ref[:, 2 * D + h * HD:2 * D + (h + 1) * HD]
        qs = q * jax.nn.sigmoid(q)
        fg = lb + (1.0 - lb) * jax.nn.sigmoid(f)
        k = jnp.where(valid, 1.0 - fg, 0.0)
        g = jnp.where(valid, jnp.log2(fg), 0.0)
        G = g
        sh = 1
        while sh < C:
            G = G + jnp.where(rloc >= sh, pltpu.roll(G, sh, 0), 0.0)
            sh *= 2
        vb = v.astype(BF16)
        A = msk_ref[0] * lax.dot_general(qs.astype(BF16), k.astype(BF16), nt,
                                         preferred_element_type=F32)
        for lev in range(nlev):
            b = 1 << lev
            if b >= 8:
                gm = G.reshape(C // (2 * b), 2 * b, HD)[:, b - 1:b, :]
                gmid = jnp.broadcast_to(gm, (C // (2 * b), 2 * b, HD)).reshape(C, HD)
            else:
                g8 = G.reshape(C // 8, 8, HD)
                sub = lax.broadcasted_iota(jnp.int32, (C // 8, 8, HD), 1)
                gmid = None
                for base in range(0, 8, 2 * b):
                    row = jnp.broadcast_to(g8[:, base + b - 1:base + b, :], (C // 8, 8, HD))
                    gmid = row if gmid is None else jnp.where(sub >= base, row, gmid)
                gmid = gmid.reshape(C, HD)
            right = (rloc // b) % 2 == 1
            fac = jnp.exp2(jnp.where(right, G - gmid, gmid - G))
            qt = jnp.where(right, qs * fac, 0.0).astype(BF16)
            kt = jnp.where(right, 0.0, k * fac).astype(BF16)
            A = A + msk_ref[lev + 1] * lax.dot_general(qt, kt, nt, preferred_element_type=F32)
        g_end = G[C - 1:C, :]
        qh = (qs * jnp.exp2(G)).astype(BF16)
        kh = (k * jnp.exp2(g_end - G)).astype(BF16)
        S = s_ref[0, h]
        o_h = (jnp.dot(A.astype(BF16), vb, preferred_element_type=F32)
               + jnp.dot(qh, S.astype(BF16), preferred_element_type=F32))
        e_col = jnp.broadcast_to(jnp.exp2(g_end), (HD, HD)).T
        s_ref[0, h] = e_col * S + lax.dot_general(kh, vb, tn, preferred_element_type=F32)
        o_heads.append(o_h)
    o = jnp.concatenate(o_heads, axis=1)
    gate = proj_ref[:, 3 * D:4 * D]
    y_ref[...] = _rms(o * jax.nn.sigmoid(gate), nw_ref[...]).astype(BF16)


def hgrn_mixer(proj, lb, nw, s0, nb, C, n_chunks, n_valid, s0_per_batch):
    msk = jnp.asarray(_hgrn_masks(C))
    nlev1 = msk.shape[0]
    s0_map = (lambda b, c: (b, 0, 0, 0)) if s0_per_batch else (lambda b, c: (0, 0, 0, 0))
    return pl.pallas_call(
        functools.partial(_hgrn_kernel, C=C, n_valid=n_valid),
        grid=(nb, n_chunks),
        in_specs=[pl.BlockSpec((C, 4 * D), lambda b, c: (b * n_chunks + c, 0)),
                  pl.BlockSpec((1, D), lambda b, c: (0, 0)),
                  pl.BlockSpec((1, D), lambda b, c: (0, 0)),
                  pl.BlockSpec((nlev1, C, C), lambda b, c: (0, 0, 0)),
                  pl.BlockSpec((1, H, HD, HD), s0_map)],
        out_specs=[pl.BlockSpec((C, D), lambda b, c: (b * n_chunks + c, 0)),
                   pl.BlockSpec((1, H, HD, HD), lambda b, c: (b, 0, 0, 0))],
        out_shape=[jax.ShapeDtypeStruct((nb * n_chunks * C, D), BF16),
                   jax.ShapeDtypeStruct((nb, H, HD, HD), F32)],
        compiler_params=_cparams(("parallel", "arbitrary")),
        name="hgrn_mixer",
    )(proj, lb.reshape(1, D), nw.reshape(1, D), msk, s0)


def _bucket_of_distance(dist):
    n = np.maximum(np.asarray(dist), 0)
    max_exact = NBUCKETS // 2
    nf = np.maximum(n, 1).astype(np.float64)
    large = max_exact + (np.log(nf / max_exact) / math.log(MAXDIST / max_exact)
                         * (NBUCKETS - max_exact)).astype(np.int64)
    return np.where(n < max_exact, n, np.minimum(large, NBUCKETS - 1))


def _bias_by_distance(dist, table):
    dist = np.asarray(dist)
    onehot = np.eye(NBUCKETS, dtype=np.float32)[_bucket_of_distance(dist).reshape(-1)]
    out = jnp.einsum("nb,bh->hn", jnp.asarray(onehot), table.astype(F32),
                     precision=lax.Precision.HIGHEST)
    return out.reshape((H,) + dist.shape)


def _toeplitz(fvals, T):
    w = jnp.pad(fvals[:, ::-1], ((0, 0), (0, 1)))
    flat = jnp.tile(w, (1, T))[:, :T * (2 * T - 1)]
    return flat.reshape(H, T, 2 * T - 1)[:, :, T - 1:]


def _subln(d, w):
    return _rms(d, w) * (1.0 - LAMBDA_INIT)


def _attn_prompt_kernel(lam_ref, q_ref, k_ref, v_ref, strip_ref, sw_ref, o_ref, s_scr, m_scr, vx_scr):
    T = AT_T
    nq = LA // T
    i = pl.program_id(2)
    nt = (((1,), (1,)), ((), ()))

    @pl.when(i == 0)
    def _():
        vx_scr[:, :HD] = v_ref[0, 0]
        vx_scr[:, HD:] = jnp.ones((LA, HD), BF16)

    def step(t):
        slot = t % 2
        if t < nq:
            qb = q_ref[0, 0]
            lane = lax.broadcasted_iota(jnp.int32, (T, HD), 1)
            zero = jnp.zeros_like(qb)
            qq = jnp.concatenate([jnp.where(lane < DQK, qb, zero), jnp.where(lane >= DQK, qb, zero)], axis=0)
            mx = jnp.full((2 * T, HD), NEG, F32)
        if t > 0:
            m_prev = m_scr[1 - slot]
            acc = jnp.zeros((2 * T, 2 * HD), F32)
        for c in range(t + 1):
            if t < nq:
                bt = strip_ref[0, min(t - c, 2)]
                sc = lax.dot_general(qq, k_ref[0, 0, c * T:(c + 1) * T, :], nt,
                                     preferred_element_type=F32) + jnp.concatenate([bt, bt], axis=0)
                s_scr[slot, c] = sc
                mx = jnp.maximum(mx, jnp.maximum(sc[:, :HD], sc[:, HD:]))
            if t > 0 and c < t:
                sp = s_scr[1 - slot, c]
                p = jnp.exp(jnp.concatenate([sp[:, :HD] - m_prev, sp[:, HD:] - m_prev], axis=1))
                acc = acc + jnp.dot(p.astype(BF16), vx_scr[c * T:(c + 1) * T, :],
                                    preferred_element_type=F32)
        if t < nq:
            m_scr[slot] = jnp.broadcast_to(jnp.max(mx, axis=-1, keepdims=True), (2 * T, HD))
        if t > 0:
            o = acc[:, :HD] / acc[:, HD:]
            o_ref[...] = _subln(o[:T] - lam_ref[0] * o[T:], sw_ref[...])

    for t in range(nq + 1):
        pl.when(i == t)(functools.partial(step, t))


def attn_prompt(q, k, v, strips, lam, sw):
    T = AT_T
    nq = LA // T
    smem = pl.BlockSpec(memory_space=pltpu.SMEM)
    return pl.pallas_call(
        _attn_prompt_kernel,
        grid=(B, H, nq + 1),
        in_specs=[smem,
                  pl.BlockSpec((1, 1, T, HD), lambda b, h, i: (b, h, jnp.minimum(i, nq - 1), 0)),
                  pl.BlockSpec((1, 1, LA, HD), lambda b, h, i: (b, h, 0, 0)),
                  pl.BlockSpec((1, 1, LA, HD), lambda b, h, i: (b, h, 0, 0)),
                  pl.BlockSpec((1, 3, T, T), lambda b, h, i: (h, 0, 0, 0)),
                  pl.BlockSpec((1, HD), lambda b, h, i: (0, 0))],
        out_specs=pl.BlockSpec((T, HD), lambda b, h, i: (b * nq + jnp.maximum(i - 1, 0), h)),
        out_shape=jax.ShapeDtypeStruct((B * LA, D), F32),
        scratch_shapes=[pltpu.VMEM((2, nq, 2 * T, T), F32), pltpu.VMEM((2, 2 * T, HD), F32),
                        pltpu.VMEM((LA, 2 * HD), BF16)],
        compiler_params=_cparams(("parallel", "parallel", "arbitrary")),
        name="attn_prompt",
    )(lam, q, k, v, strips, sw.reshape(1, HD))


def _attn_decode_kernel(pt_ref, lam_ref, q_ref, *refs):
    del pt_ref
    k_refs = refs[:PP]
    v_refs = refs[PP:2 * PP]
    bias_ref, kn_ref, vn_ref, bn_ref, sw_ref, o_ref, m_ref, acc_ref = refs[2 * PP:]
    g = pl.program_id(1)
    last = pl.num_programs(1) - 1
    nt = (((1,), (1,)), ((), ()))

    @pl.when(g == 0)
    def _():
        m_ref[...] = jnp.full(m_ref.shape, NEG, F32)
        acc_ref[...] = jnp.zeros(acc_ref.shape, F32)

    qrows = q_ref[0]
    kt = jnp.concatenate([k_refs[j][...].astype(BF16) for j in range(PP)], axis=1)
    s = jnp.dot(qrows, kt, preferred_element_type=F32) + bias_ref[0]
    ones = jnp.ones((PP * PAGE, HD), BF16)

    def finish(s_new):
        smax = s[:, :PAGE]
        for j in range(1, PP):
            smax = jnp.maximum(smax, s[:, j * PAGE:(j + 1) * PAGE])
        m_cur = jnp.max(smax, axis=-1, keepdims=True)
        if s_new is not None:
            m_cur = jnp.maximum(m_cur, jnp.max(s_new, axis=-1, keepdims=True))
        m_old = m_ref[...]
        m_new = jnp.maximum(m_old, jnp.broadcast_to(m_cur, (HD, HD)))
        m_ref[...] = m_new
        alpha = jnp.exp(m_old - m_new)
        alpha2 = jnp.concatenate([alpha, alpha], axis=1)
        p = jnp.exp(s - jnp.concatenate([m_new] * PP, axis=1)).astype(BF16)
        if s_new is not None:
            p_new = jnp.exp(s_new - m_new[:, :16]).astype(BF16)
        for hh in range(H):
            rows = slice(hh * 16, (hh + 1) * 16)
            vh = jnp.concatenate([v_refs[j][pl.ds(hh, PAGE, stride=H), :].astype(BF16)
                                  for j in range(PP)], axis=0)
            pv = jnp.dot(p[rows], jnp.concatenate([vh, ones], axis=1), preferred_element_type=F32)
            if s_new is not None:
                vn = vn_ref[0, :, hh * HD:(hh + 1) * HD].astype(BF16)
                pv = pv + jnp.dot(p_new[rows], jnp.concatenate([vn, ones[:16]], axis=1),
                                  preferred_element_type=F32)
            acc_ref[rows, :] = alpha2[rows] * acc_ref[rows, :] + pv

    @pl.when(g < last)
    def _():
        finish(None)

    @pl.when(g == last)
    def _():
        finish(lax.dot_general(qrows, kn_ref[0].astype(BF16), nt,
                               preferred_element_type=F32) + bn_ref[...])
        acc = acc_ref[...]
        o = acc[:, :HD] / acc[:, HD:]
        outs = []
        for hh in range(H):
            o1 = o[hh * 16:hh * 16 + 8]
            o2 = o[hh * 16 + 8:hh * 16 + 16]
            outs.append(_subln(o1 - lam_ref[0] * o2, sw_ref[...]))
        o_ref[0] = jnp.concatenate(outs, axis=1)


def attn_decode(page_table, lam, qrows, cache_kt, cache_v2, bias_pages, k_new, v_new, bias_new, sw):
    n_steps = NPAGES // PP

    def page_spec(j):
        return pl.BlockSpec((D, PAGE), lambda b, g, pt: (pt[b * NPAGES + g * PP + j], 0))

    smem = pl.BlockSpec(memory_space=pltpu.SMEM)
    grid_spec = pltpu.PrefetchScalarGridSpec(
        num_scalar_prefetch=1,
        grid=(DB, n_steps),
        in_specs=([smem, pl.BlockSpec((1, HD, D), lambda b, g, pt: (b, 0, 0))]
                  + [page_spec(j) for j in range(PP)]
                  + [page_spec(j) for j in range(PP)]
                  + [pl.BlockSpec((1, HD, PP * PAGE), lambda b, g, pt: (g, 0, 0)),
                     pl.BlockSpec((1, 16, D), lambda b, g, pt: (b, 0, 0)),
                     pl.BlockSpec((1, 16, D), lambda b, g, pt: (b, 0, 0)),
                     pl.BlockSpec((HD, 16), lambda b, g, pt: (0, 0)),
                     pl.BlockSpec((1, HD), lambda b, g, pt: (0, 0))]),
        out_specs=pl.BlockSpec((1, 8, D), lambda b, g, pt: (b, 0, 0)),
        scratch_shapes=[pltpu.VMEM((HD, HD), F32), pltpu.VMEM((HD, 2 * HD), F32)],
    )
    return pl.pallas_call(
        _attn_decode_kernel,
        grid_spec=grid_spec,
        out_shape=jax.ShapeDtypeStruct((DB, 8, D), F32),
        compiler_params=_cparams(("parallel", "arbitrary")),
        name="attn_decode",
    )(page_table.reshape(-1), lam, qrows, *([cache_kt] * PP), *([cache_v2] * PP),
      bias_pages, k_new, v_new, bias_new, sw.reshape(1, HD))


def _router_kernel(x_ref, wn_ref, wr_ref, br_ref, h_ref, idx_ref, gate_ref):
    h = _rms(x_ref[...], wn_ref[...])
    h_ref[...] = h
    h_hi = h.astype(BF16)
    h_lo = (h - h_hi.astype(F32)).astype(BF16)
    logits = (jnp.dot(h_hi, wr_ref[0], preferred_element_type=F32)
              + jnp.dot(h_lo, wr_ref[0], preferred_element_type=F32)
              + jnp.dot(h_hi, wr_ref[1], preferred_element_type=F32)) + br_ref[...]
    lane = lax.broadcasted_iota(jnp.int32, logits.shape, 1)
    ninf = jnp.float32(-jnp.inf)
    logits = jnp.where(lane < NEXP, logits, ninf)
    v1 = jnp.max(logits, axis=-1, keepdims=True)
    i1 = jnp.min(jnp.where(logits == v1, lane, HD), axis=-1, keepdims=True)
    rest = jnp.where(lane == i1, ninf, logits)
    v2 = jnp.max(rest, axis=-1, keepdims=True)
    i2 = jnp.min(jnp.where(rest == v2, lane, HD), axis=-1, keepdims=True)
    e = jnp.exp(v2 - v1)
    g1 = 1.0 / (1.0 + e)
    g2 = e / (1.0 + e)
    idx_ref[...] = jnp.where(lane == 0, i1, jnp.where(lane == 1, i2, 0))
    gate_ref[...] = jnp.where(lane == 0, g1, jnp.where(lane == 1, g2, 0.0))


def moe_router(x, wn, w_router, b_router):
    rows = x.shape[0]
    wr = jnp.zeros((D, HD), F32).at[:, :NEXP].set(w_router.astype(F32))
    wr_hi = wr.astype(BF16)
    wr_lo = (wr - wr_hi.astype(F32)).astype(BF16)
    br = jnp.zeros((1, HD), F32).at[0, :NEXP].set(b_router.astype(F32))
    return pl.pallas_call(
        _router_kernel,
        grid=(rows // TM,),
        in_specs=[pl.BlockSpec((TM, D), lambda i: (i, 0)),
                  pl.BlockSpec((1, D), lambda i: (0, 0)),
                  pl.BlockSpec((2, D, HD), lambda i: (0, 0, 0)),
                  pl.BlockSpec((1, HD), lambda i: (0, 0))],
        out_specs=[pl.BlockSpec((TM, D), lambda i: (i, 0)),
                   pl.BlockSpec((TM, HD), lambda i: (i, 0)),
                   pl.BlockSpec((TM, HD), lambda i: (i, 0))],
        out_shape=[jax.ShapeDtypeStruct((rows, D), F32),
                   jax.ShapeDtypeStruct((rows, HD), jnp.int32),
                   jax.ShapeDtypeStruct((rows, HD), F32)],
        compiler_params=_cparams(("parallel",)),
        name="moe_router",
    )(x, wn.reshape(1, D), jnp.stack([wr_hi, wr_lo]), br)


def _moe_kernel(te_ref, nv_ref, sa_ref, h_hbm, wa_ref, wb_ref, wd_ref, out_hbm,
                hbuf, hb16, obuf, gsem, ssem):
    del te_ref
    i = pl.program_id(0)
    j = pl.program_id(1)
    nt = pl.num_programs(0)
    last_j = pl.num_programs(1) - 1
    slot = lax.rem(i, 2)
    live = nv_ref[i] > 0

    def start_gather(tile, s):
        def body(r, carry):
            a = sa_ref[tile * MOE_TM + r]
            row = jnp.minimum(jnp.where(a < R, a, a - R), R - 1)
            pltpu.make_async_copy(h_hbm.at[pl.ds(row, 1)], hbuf.at[s, pl.ds(r, 1)], gsem.at[s]).start()
            return carry
        lax.fori_loop(0, MOE_TM, body, 0, unroll=8)

    def wait_gather(s):
        pltpu.make_async_copy(h_hbm.at[pl.ds(0, MOE_TM)], hbuf.at[s], gsem.at[s]).wait()

    def start_scatter(tile, s):
        def body(r, carry):
            a = sa_ref[tile * MOE_TM + r]
            pltpu.make_async_copy(obuf.at[s, pl.ds(r, 1)], out_hbm.at[pl.ds(a, 1)], ssem.at[s]).start()
            return carry
        lax.fori_loop(0, MOE_TM, body, 0, unroll=8)

    def wait_scatter(s):
        pltpu.make_async_copy(obuf.at[s], out_hbm.at[pl.ds(0, MOE_TM)], ssem.at[s]).wait()

    @pl.when(j == 0)
    def _():
        @pl.when((i >= 2) & (nv_ref[jnp.maximum(i - 2, 0)] > 0))
        def _():
            wait_scatter(slot)

        @pl.when(live)
        def _():
            @pl.when(i == 0)
            def _():
                start_gather(0, 0)
                obuf[1] = jnp.zeros((MOE_TM, D), F32)
                for blk in range(2):
                    pltpu.make_async_copy(obuf.at[1], out_hbm.at[pl.ds(2 * R + blk * MOE_TM, MOE_TM)],
                                          ssem.at[1]).start()
                for blk in range(2):
                    wait_scatter(1)

            nxt = jnp.minimum(i + 1, nt - 1)

            @pl.when((i + 1 < nt) & (nv_ref[nxt] > 0))
            def _():
                start_gather(nxt, 1 - slot)

            wait_gather(slot)
            hb16[...] = hbuf[slot].astype(BF16)

    @pl.when(live)
    def _():
        h = hb16[...]
        a = jnp.dot(h, wa_ref[0], preferred_element_type=F32)
        b = jnp.dot(h, wb_ref[0], preferred_element_type=F32)
        act = (a * jax.nn.sigmoid(a) * b).astype(BF16)
        y = jnp.dot(act, wd_ref[0], preferred_element_type=F32)

        @pl.when(j == 0)
        def _():
            obuf[slot] = y

        @pl.when(j > 0)
        def _():
            obuf[slot] += y

        @pl.when(j == last_j)
        def _():
            start_scatter(i, slot)

    @pl.when((i == nt - 1) & (j == last_j))
    def _():
        @pl.when(nv_ref[nt - 2] > 0)
        def _():
            wait_scatter(1 - slot)

        @pl.when(live)
        def _():
            wait_scatter(slot)


def moe_experts(tile_expert, tile_rows, slot_assign, h, w_up, w_down):
    nf = DFFE // MOE_FC
    grid_spec = pltpu.PrefetchScalarGridSpec(
        num_scalar_prefetch=3,
        grid=(MOE_NT, nf),
        in_specs=[pl.BlockSpec(memory_space=pl.ANY),
                  pl.BlockSpec((1, D, MOE_FC), lambda i, j, te, nv, sa: (te[i], 0, j)),
                  pl.BlockSpec((1, D, MOE_FC), lambda i, j, te, nv, sa: (te[i], 0, nf + j)),
                  pl.BlockSpec((1, MOE_FC, D), lambda i, j, te, nv, sa: (te[i], j, 0))],
        out_specs=pl.BlockSpec(memory_space=pl.ANY),
        scratch_shapes=[pltpu.VMEM((2, MOE_TM, D), F32), pltpu.VMEM((MOE_TM, D), BF16),
                        pltpu.VMEM((2, MOE_TM, D), F32),
                        pltpu.SemaphoreType.DMA((2,)), pltpu.SemaphoreType.DMA((2,))],
    )
    return pl.pallas_call(
        _moe_kernel,
        grid_spec=grid_spec,
        out_shape=jax.ShapeDtypeStruct((2 * R + 2 * MOE_TM, D), F32),
        compiler_params=_cparams(("arbitrary", "arbitrary")),
        name="moe_experts",
    )(tile_expert, tile_rows, slot_assign, h, w_up, w_up, w_down)


def _combine_kernel(x_ref, g_ref, e0_ref, e1_ref, wn_ref, y_ref):
    moe = g_ref[:, 0:1] * e0_ref[...] + g_ref[:, 1:2] * e1_ref[...]
    y_ref[...] = _rms(x_ref[...] + moe, wn_ref[...])


def moe_combine_norm(x, gates, eo, wn):
    rows = x.shape[0]
    nb = rows // TM
    return pl.pallas_call(
        _combine_kernel,
        grid=(rows // TM,),
        in_specs=[pl.BlockSpec((TM, D), lambda i: (i, 0)),
                  pl.BlockSpec((TM, HD), lambda i: (i, 0)),
                  pl.BlockSpec((TM, D), lambda i: (i, 0)),
                  pl.BlockSpec((TM, D), lambda i: (nb + i, 0)),
                  pl.BlockSpec((1, D), lambda i: (0, 0))],
        out_specs=pl.BlockSpec((TM, D), lambda i: (i, 0)),
        out_shape=jax.ShapeDtypeStruct((rows, D), F32),
        compiler_params=_cparams(("parallel",)),
        name="moe_combine_norm",
    )(x, gates, eo, eo, wn.reshape(1, D))


def _moe_plan(idx):
    e_flat = idx.reshape(-1)
    onehot = (e_flat[:, None] == jnp.arange(NEXP, dtype=jnp.int32)[None, :]).astype(jnp.int32)
    cum = jnp.cumsum(onehot, axis=0)
    counts = cum[-1]
    padded = ((counts + MOE_TM - 1) // MOE_TM) * MOE_TM
    ends = jnp.cumsum(padded)
    starts = ends - padded
    slot = jnp.sum(onehot * (starts[None, :] + cum - onehot), axis=1)
    pos = jnp.arange(MOE_NT * MOE_TM, dtype=jnp.int32)
    spare = 2 * R + ((pos // MOE_TM) % 2) * MOE_TM + pos % MOE_TM
    flat = jnp.arange(2 * R, dtype=jnp.int32)
    slot_assign = spare.at[slot].set((flat % 2) * R + flat // 2)
    tile_start = jnp.arange(MOE_NT, dtype=jnp.int32) * MOE_TM
    owner = (tile_start[:, None] >= starts[None, :]) & (tile_start[:, None] < ends[None, :])
    tile_rows = jnp.sum(jnp.where(owner, jnp.clip(counts[None, :] - (tile_start[:, None] - starts[None, :]),
                                                  0, MOE_TM), 0), axis=1).astype(jnp.int32)
    last_expert = jnp.max(jnp.where(counts > 0, jnp.arange(NEXP, dtype=jnp.int32), 0))
    tile_expert = jnp.where(jnp.any(owner, axis=1), jnp.argmax(owner, axis=1), last_expert).astype(jnp.int32)
    return tile_expert, tile_rows, slot_assign


def kernel(x_prompt, x_sample, state_hgrn, cache_k, cache_v, page_table, meta_tokens,
           norm_mix_w, norm_ffn_w, hg_w_in, hg_lower_bound, hg_norm_w, hg_w_out,
           at_w_qkv, at_lambda_q1, at_lambda_k1, at_lambda_q2, at_lambda_k2, at_subln_w,
           at_w_out, rel_bias_table, ff_w_up, ff_w_down, moe_w_router, moe_b_router,
           moe_w_up, moe_w_down, final_norm_w):
    xp = jnp.concatenate([jnp.broadcast_to(meta_tokens.astype(F32), (B, NMETA, D)), x_prompt,
                          jnp.zeros((B, LP - LREAL, D), F32)], axis=1)
    x0 = jnp.concatenate([xp.reshape(RP, D), x_sample.reshape(RS, D)], axis=0)
    lower = jnp.cumsum(jax.nn.softmax(hg_lower_bound.astype(F32), axis=0), axis=0)

    proj = norm_matmul(x0, norm_mix_w[0], hg_w_in[0].astype(BF16), 1024)
    y_p, st_p = hgrn_mixer(proj, lower[0], hg_norm_w[0], jnp.zeros((1, H, HD, HD), F32),
                           B, HG_C, LP // HG_C, LREAL, False)
    proj_s = jnp.pad(proj[RP:].reshape(DB, DT, 4 * D), ((0, 0), (0, HG_CS - DT), (0, 0)))
    y_s, st_s = hgrn_mixer(proj_s.reshape(DB * HG_CS, 4 * D), lower[0], hg_norm_w[0], state_hgrn[0],
                           DB, HG_CS, 1, DT, True)
    y0 = jnp.concatenate([y_p, y_s.reshape(DB, HG_CS, D)[:, :DT].reshape(RS, D)], axis=0)
    x1 = matmul_residual(y0, hg_w_out[0].astype(BF16), x0)
    x2 = dense_ffn(x1, norm_ffn_w[0], ff_w_up[0].astype(BF16), ff_w_down[0].astype(BF16))

    f = F32
    lam = (jnp.exp(jnp.sum(at_lambda_q1[0].astype(f) * at_lambda_k1[0].astype(f)))
           - jnp.exp(jnp.sum(at_lambda_q2[0].astype(f) * at_lambda_k2[0].astype(f))) + LAMBDA_INIT)
    lam = lam.reshape(1).astype(F32)
    qkv = norm_matmul(x2, norm_mix_w[1], at_w_qkv[0].astype(BF16), 1024)
    q_all, k_all, v_all = qkv[:, :D], qkv[:, D:2 * D], qkv[:, 2 * D:]
    qscale = DQK ** -0.5

    def heads_major(t, scale=1.0):
        t = (t[:RP] * scale).reshape(B, LP, H, HD).transpose(0, 2, 1, 3).astype(BF16)
        return jnp.pad(t, ((0, 0), (0, 0), (0, LA - LP), (0, 0)))

    T = AT_T
    dvals = np.arange(-(T - 1), T)
    f_diag = jnp.where(jnp.asarray(dvals >= 0)[None, :], _bias_by_distance(dvals, rel_bias_table), NEG)
    f_prev = _bias_by_distance(dvals + T, rel_bias_table)
    cfar = rel_bias_table[NBUCKETS - 1].astype(F32)
    strips = jnp.stack([_toeplitz(f_diag, T), _toeplitz(f_prev, T),
                        jnp.broadcast_to(cfar[:, None, None], (H, T, T))], axis=1)
    o_p = attn_prompt(heads_major(q_all, qscale), heads_major(k_all), heads_major(v_all),
                      strips, lam, at_subln_w[0])

    q_s = (q_all[RP:] * qscale).reshape(DB, DT, H, 2, DQK)
    sel = jnp.einsum("hj,mn->hmjn", jnp.eye(H, dtype=F32), jnp.eye(2, dtype=F32))
    qrows = jnp.einsum("bthmd,hmjn->bhmtjnd", q_s, sel)
    qrows = jnp.pad(qrows, ((0, 0),) * 3 + ((0, 8 - DT),) + ((0, 0),) * 3)
    qrows = qrows.reshape(DB, HD, D).astype(BF16)
    tq = np.minimum(np.arange(8), DT - 1)
    kpos = np.arange(PAST).reshape(NPAGES, 1, PAGE)
    bp = _bias_by_distance(PAST + tq[None, :, None] - kpos, rel_bias_table)
    bias_pages = jnp.broadcast_to(bp.transpose(1, 0, 2, 3)[:, :, None], (NPAGES, H, 2, 8, PAGE))
    bias_pages = bias_pages.reshape(NPAGES // PP, PP, HD, PAGE).transpose(0, 2, 1, 3)
    bias_pages = bias_pages.reshape(NPAGES // PP, HD, PP * PAGE)
    sn = np.arange(16)
    dn = tq[:, None] - sn[None, :]
    bn = jnp.where(jnp.asarray((dn >= 0) & (sn[None, :] < DT))[None],
                   _bias_by_distance(dn, rel_bias_table), NEG)
    bias_new = jnp.broadcast_to(bn[:, None], (H, 2, 8, 16)).reshape(HD, 16)
    k_new = jnp.pad(k_all[RP:].reshape(DB, DT, D), ((0, 0), (0, 16 - DT), (0, 0)))
    v_new = jnp.pad(v_all[RP:].reshape(DB, DT, D), ((0, 0), (0, 16 - DT), (0, 0)))
    n_phys = cache_k.shape[1]
    cache_kt = jnp.transpose(cache_k[0], (0, 2, 3, 4, 1)).reshape(n_phys * D, PAGE)
    cache_v2 = cache_v[0].reshape(n_phys * PAGE * H, HD)
    o_s = attn_decode(page_table, lam, qrows, cache_kt, cache_v2, bias_pages, k_new, v_new, bias_new,
                      at_subln_w[0])
    o1 = jnp.concatenate([o_p.reshape(B, LA, D)[:, :LP].reshape(RP, D),
                          o_s[:, :DT].reshape(RS, D)], axis=0)
    x3 = matmul_residual(o1, at_w_out[0].astype(BF16), x2)

    h_moe, idx, gates = moe_router(x3, norm_ffn_w[1], moe_w_router[0], moe_b_router[0])
    tile_expert, tile_rows, slot_assign = _moe_plan(idx[:, :TOPK])
    eo = moe_experts(tile_expert, tile_rows, slot_assign, h_moe,
                     moe_w_up[0].astype(BF16), moe_w_down[0].astype(BF16))
    y = moe_combine_norm(x3, gates, eo, final_norm_w)

    y_prompt = y[:RP].reshape(B, LP, D)[:, NMETA:LREAL]
    y_sample = y[RP:].reshape(DB, DT, D)
    k_p = k_all[:RP].reshape(B, LP, H, 2, DQK)[:, :LREAL]
    v_p = v_all[:RP].reshape(B, LP, H, HD)[:, :LREAL]
    k_s = k_all[RP:].reshape(DB, DT, H, 2, DQK)
    v_s = v_all[RP:].reshape(DB, DT, H, HD)
    return (y_prompt, y_sample, st_p[None], st_s[None], k_p[None], v_p[None], k_s[None], v_s[None])
```

```python
import functools
import math

import numpy as np
import jax
import jax.numpy as jnp
from jax import lax
from jax.experimental import pallas as pl
from jax.experimental.pallas import tpu as pltpu

F32 = jnp.float32
BF16 = jnp.bfloat16

D = 1024
B = 8
SEQ = 2048
DB = 128
DT = 4
PAST = 2048
PAGE = 128
NPAGES = PAST // PAGE
NMETA = 16
LREAL = NMETA + SEQ
LP = 2176
RP = B * LP
RS = DB * DT
R = RP + RS
H = 8
HD = 128
DQK = 64
NBUCKETS = 32
MAXDIST = 128
DFF = 2816
NEXP = 8
TOPK = 2
DFFE = 3584
EPS = 1e-6
NEG = -1e30
LAMBDA_INIT = 0.8 - 0.6 * math.exp(-0.3 * 1)

VMEM_LIMIT = 56 * 1024 * 1024

TM = 512
HG_C = 128
HG_CS = 16
AT_T = 256
LA = 2304
MOE_TM = 512
MOE_NT = (2 * R) // MOE_TM + NEXP
MOE_FC = 1792
FFN_FC = 1408
PP = 8


def _cparams(sem, vmem=VMEM_LIMIT):
    return pltpu.CompilerParams(dimension_semantics=sem, vmem_limit_bytes=vmem)


def _rms(x, w):
    return (x * lax.rsqrt(jnp.mean(x * x, axis=-1, keepdims=True) + EPS)) * w


def _norm_matmul_kernel(x_ref, wn_ref, w_ref, o_ref, h_ref):
    @pl.when(pl.program_id(1) == 0)
    def _():
        h_ref[...] = _rms(x_ref[...], wn_ref[...]).astype(BF16)

    o_ref[...] = jnp.dot(h_ref[...], w_ref[...], preferred_element_type=F32)


def norm_matmul(x, wn, w, tn):
    rows, n = x.shape[0], w.shape[1]
    return pl.pallas_call(
        _norm_matmul_kernel,
        grid=(rows // TM, n // tn),
        in_specs=[pl.BlockSpec((TM, D), lambda i, j: (i, 0)),
                  pl.BlockSpec((1, D), lambda i, j: (0, 0)),
                  pl.BlockSpec((D, tn), lambda i, j: (0, j))],
        out_specs=pl.BlockSpec((TM, tn), lambda i, j: (i, j)),
        out_shape=jax.ShapeDtypeStruct((rows, n), F32),
        scratch_shapes=[pltpu.VMEM((TM, D), BF16)],
        compiler_params=_cparams(("parallel", "arbitrary")),
        name="norm_matmul",
    )(x, wn.reshape(1, D), w)


def _matmul_residual_kernel(a_ref, w_ref, x_ref, o_ref):
    o_ref[...] = x_ref[...] + jnp.dot(a_ref[...].astype(BF16), w_ref[...],
                                      preferred_element_type=F32)


def matmul_residual(a, w, x):
    rows = x.shape[0]
    return pl.pallas_call(
        _matmul_residual_kernel,
        grid=(rows // TM,),
        in_specs=[pl.BlockSpec((TM, D), lambda i: (i, 0)),
                  pl.BlockSpec((D, D), lambda i: (0, 0)),
                  pl.BlockSpec((TM, D), lambda i: (i, 0))],
        out_specs=pl.BlockSpec((TM, D), lambda i: (i, 0)),
        out_shape=jax.ShapeDtypeStruct((rows, D), F32),
        compiler_params=_cparams(("parallel",)),
        name="matmul_residual",
    )(a, w, x)


def _ffn_kernel(x_ref, wn_ref, wa_ref, wb_ref, wd_ref, o_ref, h_ref):
    j = pl.program_id(1)

    @pl.when(j == 0)
    def _():
        h_ref[...] = _rms(x_ref[...], wn_ref[...]).astype(BF16)

    h = h_ref[...]
    a = jnp.dot(h, wa_ref[...], preferred_element_type=F32)
    b = jnp.dot(h, wb_ref[...], preferred_element_type=F32)
    act = (a * jax.nn.sigmoid(a) * b).astype(BF16)
    y = jnp.dot(act, wd_ref[...], preferred_element_type=F32)

    @pl.when(j == 0)
    def _():
        o_ref[...] = x_ref[...] + y

    @pl.when(j > 0)
    def _():
        o_ref[...] += y


def dense_ffn(x, wn, w_up, w_down):
    rows = x.shape[0]
    nf = DFF // FFN_FC
    return pl.pallas_call(
        _ffn_kernel,
        grid=(rows // TM, nf),
        in_specs=[pl.BlockSpec((TM, D), lambda i, j: (i, 0)),
                  pl.BlockSpec((1, D), lambda i, j: (0, 0)),
                  pl.BlockSpec((D, FFN_FC), lambda i, j: (0, j)),
                  pl.BlockSpec((D, FFN_FC), lambda i, j: (0, nf + j)),
                  pl.BlockSpec((FFN_FC, D), lambda i, j: (j, 0))],
        out_specs=pl.BlockSpec((TM, D), lambda i, j: (i, 0)),
        out_shape=jax.ShapeDtypeStruct((rows, D), F32),
        scratch_shapes=[pltpu.VMEM((TM, D), BF16)],
        compiler_params=_cparams(("parallel", "arbitrary")),
        name="dense_ffn",
    )(x, wn.reshape(1, D), w_up, w_up, w_down)


def _hgrn_masks(C):
    nlev = int(math.log2(C))
    t = np.arange(C)[:, None]
    s = np.arange(C)[None, :]
    m = [(t == s)]
    for lev in range(nlev):
        b = 1 << lev
        m.append((t // (2 * b) == s // (2 * b)) & ((t // b) % 2 == 1) & ((s // b) % 2 == 0))
    return np.stack(m).astype(np.float32)


def _hgrn_kernel(proj_ref, lb_ref, nw_ref, msk_ref, s0_ref, y_ref, s_ref, *, C, n_valid):
    nlev = int(math.log2(C))
    c = pl.program_id(1)

    @pl.when(c == 0)
    def _():
        s_ref[...] = s0_ref[...]

    rloc = lax.broadcasted_iota(jnp.int32, (C, HD), 0)
    valid = (c * C + rloc) < n_valid
    nt = (((1,), (1,)), ((), ()))
    tn = (((0,), (0,)), ((), ()))
    o_heads = []
    for h in range(H):
        col = slice(h * HD, (h + 1) * HD)
        lb = lb_ref[:, col]
        q = proj_ref[:, h * HD:(h + 1) * HD]
        f = proj_ref[:, D + h * HD:D + (h + 1) * HD]
        v = proj_ref[:, 2 * D + h * HD:2 * D + (h + 1) * HD]
        qs = q * jax.nn.sigmoid(q)
        fg = lb + (1.0 - lb) * jax.nn.sigmoid(f)
        k = jnp.where(valid, 1.0 - fg, 0.0)
        g = jnp.where(valid, jnp.log2(fg), 0.0)
        G = g
        sh = 1
        while sh < C:
            G = G + jnp.where(rloc >= sh, pltpu.roll(G, sh, 0), 0.0)
            sh *= 2
        vb = v.astype(BF16)
        A = msk_ref[0] * lax.dot_general(qs.astype(BF16), k.astype(BF16), nt,
                                         preferred_element_type=F32)
        for lev in range(nlev):
            b = 1 << lev
            if b >= 8:
                gm = G.reshape(C // (2 * b), 2 * b, HD)[:, b - 1:b, :]
                gmid = jnp.broadcast_to(gm, (C // (2 * b), 2 * b, HD)).reshape(C, HD)
            else:
                g8 = G.reshape(C // 8, 8, HD)
                sub = lax.broadcasted_iota(jnp.int32, (C // 8, 8, HD), 1)
                gmid = None
                for base in range(0, 8, 2 * b):
                    row = jnp.broadcast_to(g8[:, base + b - 1:base + b, :], (C // 8, 8, HD))
                    gmid = row if gmid is None else jnp.where(sub >= base, row, gmid)
                gmid = gmid.reshape(C, HD)
            right = (rloc // b) % 2 == 1
            fac = jnp.exp2(jnp.where(right, G - gmid, gmid - G))
            qt = jnp.where(right, qs * fac, 0.0).astype(BF16)
            kt = jnp.where(right, 0.0, k * fac).astype(BF16)
            A = A + msk_ref[lev + 1] * lax.dot_general(qt, kt, nt, preferred_element_type=F32)
        g_end = G[C - 1:C, :]
        qh = (qs * jnp.exp2(G)).astype(BF16)
        kh = (k * jnp.exp2(g_end - G)).astype(BF16)
        S = s_ref[0, h]
        o_h = (jnp.dot(A.astype(BF16), vb, preferred_element_type=F32)
               + jnp.dot(qh, S.astype(BF16), preferred_element_type=F32))
        e_col = jnp.broadcast_to(jnp.exp2(g_end), (HD, HD)).T
        s_ref[0, h] = e_col * S + lax.dot_general(kh, vb, tn, preferred_element_type=F32)
        o_heads.append(o_h)
    o = jnp.concatenate(o_heads, axis=1)
    gate = proj_ref[:, 3 * D:4 * D]
    y_ref[...] = _rms(o * jax.nn.sigmoid(gate), nw_ref[...]).astype(BF16)


def hgrn_mixer(proj, lb, nw, s0, nb, C, n_chunks, n_valid, s0_per_batch):
    msk = jnp.asarray(_hgrn_masks(C))
    nlev1 = msk.shape[0]
    s0_map = (lambda b, c: (b, 0, 0, 0)) if s0_per_batch else (lambda b, c: (0, 0, 0, 0))
    return pl.pallas_call(
        functools.partial(_hgrn_kernel, C=C, n_valid=n_valid),
        grid=(nb, n_chunks),
        in_specs=[pl.BlockSpec((C, 4 * D), lambda b, c: (b * n_chunks + c, 0)),
                  pl.BlockSpec((1, D), lambda b, c: (0, 0)),
                  pl.BlockSpec((1, D), lambda b, c: (0, 0)),
                  pl.BlockSpec((nlev1, C, C), lambda b, c: (0, 0, 0)),
                  pl.BlockSpec((1, H, HD, HD), s0_map)],
        out_specs=[pl.BlockSpec((C, D), lambda b, c: (b * n_chunks + c, 0)),
                   pl.BlockSpec((1, H, HD, HD), lambda b, c: (b, 0, 0, 0))],
        out_shape=[jax.ShapeDtypeStruct((nb * n_chunks * C, D), BF16),
                   jax.ShapeDtypeStruct((nb, H, HD, HD), F32)],
        compiler_params=_cparams(("parallel", "arbitrary")),
        name="hgrn_mixer",
    )(proj, lb.reshape(1, D), nw.reshape(1, D), msk, s0)


def _bucket_of_distance(dist):
    n = np.maximum(np.asarray(dist), 0)
    max_exact = NBUCKETS // 2
    nf = np.maximum(n, 1).astype(np.float64)
    large = max_exact + (np.log(nf / max_exact) / math.log(MAXDIST / max_exact)
                         * (NBUCKETS - max_exact)).astype(np.int64)
    return np.where(n < max_exact, n, np.minimum(large, NBUCKETS - 1))


def _bias_by_distance(dist, table):
    dist = np.asarray(dist)
    onehot = np.eye(NBUCKETS, dtype=np.float32)[_bucket_of_distance(dist).reshape(-1)]
    out = jnp.einsum("nb,bh->hn", jnp.asarray(onehot), table.astype(F32),
                     precision=lax.Precision.HIGHEST)
    return out.reshape((H,) + dist.shape)


def _toeplitz(fvals, T):
    w = jnp.pad(fvals[:, ::-1], ((0, 0), (0, 1)))
    flat = jnp.tile(w, (1, T))[:, :T * (2 * T - 1)]
    return flat.reshape(H, T, 2 * T - 1)[:, :, T - 1:]


def _subln(d, w):
    return _rms(d, w) * (1.0 - LAMBDA_INIT)


def _attn_prompt_kernel(lam_ref, q_ref, k_ref, v_ref, strip_ref, sw_ref, o_ref, s_scr, m_scr, vx_scr):
    T = AT_T
    nq = LA // T
    nt = (((1,), (1,)), ((), ()))
    vx_scr[:, :HD] = v_ref[0, 0]
    vx_scr[:, HD:] = jnp.ones((LA, HD), BF16)

    def step(t):
        slot = t % 2
        if t < nq:
            qb = q_ref[0, 0, t * T:(t + 1) * T, :]
            lane = lax.broadcasted_iota(jnp.int32, (T, HD), 1)
            zero = jnp.zeros_like(qb)
            qq = jnp.concatenate([jnp.where(lane < DQK, qb, zero), jnp.where(lane >= DQK, qb, zero)], axis=0)
            mx = jnp.full((2 * T, HD), NEG, F32)
        if t > 0:
            m_prev = m_scr[1 - slot]
            acc = jnp.zeros((2 * T, 2 * HD), F32)
        for c in range(t + 1):
            if t < nq:
                bt = strip_ref[0, min(t - c, 2)]
                sc = lax.dot_general(qq, k_ref[0, 0, c * T:(c + 1) * T, :], nt,
                                     preferred_element_type=F32) + jnp.concatenate([bt, bt], axis=0)
                s_scr[slot, c] = sc
                mx = jnp.maximum(mx, jnp.maximum(sc[:, :HD], sc[:, HD:]))
            if t > 0 and c < t:
                sp = s_scr[1 - slot, c]
                p = jnp.exp(jnp.concatenate([sp[:, :HD] - m_prev, sp[:, HD:] - m_prev], axis=1))
                acc = acc + jnp.dot(p.astype(BF16), vx_scr[c * T:(c + 1) * T, :],
                                    preferred_element_type=F32)
        if t < nq:
            m_scr[slot] = jnp.broadcast_to(jnp.max(mx, axis=-1, keepdims=True), (2 * T, HD))
        if t > 0:
            o = acc[:, :HD] / acc[:, HD:]
            o_ref[(t - 1) * T:t * T, :] = _subln(o[:T] - lam_ref[0] * o[T:], sw_ref[...])

    for t in range(nq + 1):
        step(t)


def attn_prompt(q, k, v, strips, lam, sw):
    T = AT_T
    nq = LA // T
    smem = pl.BlockSpec(memory_space=pltpu.SMEM)
    return pl.pallas_call(
        _attn_prompt_kernel,
        grid=(B, H),
        in_specs=[smem,
                  pl.BlockSpec((1, 1, LA, HD), lambda b, h: (b, h, 0, 0)),
                  pl.BlockSpec((1, 1, LA, HD), lambda b, h: (b, h, 0, 0)),
                  pl.BlockSpec((1, 1, LA, HD), lambda b, h: (b, h, 0, 0)),
                  pl.BlockSpec((1, 3, T, T), lambda b, h: (h, 0, 0, 0)),
                  pl.BlockSpec((1, HD), lambda b, h: (0, 0))],
        out_specs=pl.BlockSpec((LA, HD), lambda b, h: (b, h)),
        out_shape=jax.ShapeDtypeStruct((B * LA, D), F32),
        scratch_shapes=[pltpu.VMEM((2, nq, 2 * T, T), F32), pltpu.VMEM((2, 2 * T, HD), F32),
                        pltpu.VMEM((LA, 2 * HD), BF16)],
        compiler_params=_cparams(("parallel", "parallel")),
        name="attn_prompt",
    )(lam, q, k, v, strips, sw.reshape(1, HD))


def _attn_decode_kernel(pt_ref, lam_ref, q_ref, *refs):
    del pt_ref
    k_refs = refs[:PP]
    v_refs = refs[PP:2 * PP]
    bias_ref, kn_ref, vn_ref, bn_ref, sw_ref, o_ref, m_ref, acc_ref, q_scr = refs[2 * PP:]
    g = pl.program_id(1)
    last = pl.num_programs(1) - 1
    nt = (((1,), (1,)), ((), ()))

    @pl.when(g == 0)
    def _():
        m_ref[...] = jnp.full(m_ref.shape, NEG, F32)
        acc_ref[...] = jnp.zeros(acc_ref.shape, F32)
        q8 = q_ref[0]
        r = lax.broadcasted_iota(jnp.int32, (HD, D), 0)
        col = lax.broadcasted_iota(jnp.int32, (HD, D), 1)
        q_scr[...] = jnp.where(r // 8 == col // DQK, jnp.concatenate([q8] * 16, axis=0), 0.0).astype(BF16)

    qrows = q_scr[...]
    kt = jnp.concatenate([k_refs[j][...].astype(BF16) for j in range(PP)], axis=1)
    s = jnp.dot(qrows, kt, preferred_element_type=F32) + bias_ref[0]
    ones = jnp.ones((PP * PAGE, HD), BF16)

    def finish(s_new):
        smax = s[:, :PAGE]
        for j in range(1, PP):
            smax = jnp.maximum(smax, s[:, j * PAGE:(j + 1) * PAGE])
        m_cur = jnp.max(smax, axis=-1, keepdims=True)
        if s_new is not None:
            m_cur = jnp.maximum(m_cur, jnp.max(s_new, axis=-1, keepdims=True))
        m_old = m_ref[...]
        m_new = jnp.maximum(m_old, jnp.broadcast_to(m_cur, (HD, HD)))
        m_ref[...] = m_new
        alpha = jnp.exp(m_old - m_new)
        alpha2 = jnp.concatenate([alpha, alpha], axis=1)
        p = jnp.exp(s - jnp.concatenate([m_new] * PP, axis=1)).astype(BF16)
        if s_new is not None:
            p_new = jnp.exp(s_new - m_new[:, :16]).astype(BF16)
        for hh in range(H):
            rows = slice(hh * 16, (hh + 1) * 16)
            vh = jnp.concatenate([v_refs[j][pl.ds(hh, PAGE, stride=H), :].astype(BF16)
                                  for j in range(PP)], axis=0)
            pv = jnp.dot(p[rows], jnp.concatenate([vh, ones], axis=1), preferred_element_type=F32)
            if s_new is not None:
                vn = vn_ref[0, :, hh * HD:(hh + 1) * HD].astype(BF16)
                pv = pv + jnp.dot(p_new[rows], jnp.concatenate([vn, ones[:16]], axis=1),
                                  preferred_element_type=F32)
            acc_ref[rows, :] = alpha2[rows] * acc_ref[rows, :] + pv

    @pl.when(g < last)
    def _():
        finish(None)

    @pl.when(g == last)
    def _():
        finish(lax.dot_general(qrows, kn_ref[0].astype(BF16), nt,
                               preferred_element_type=F32) + bn_ref[...])
        acc = acc_ref[...]
        o = acc[:, :HD] / acc[:, HD:]
        outs = []
        for hh in range(H):
            o1 = o[hh * 16:hh * 16 + 8]
            o2 = o[hh * 16 + 8:hh * 16 + 16]
            outs.append(_subln(o1 - lam_ref[0] * o2, sw_ref[...]))
        o_ref[0] = jnp.concatenate(outs, axis=1)


def attn_decode(page_table, lam, q8, cache_kt, cache_v2, bias_pages, k_new, v_new, bias_new, sw):
    n_steps = NPAGES // PP

    def page_spec(j):
        return pl.BlockSpec((D, PAGE), lambda b, g, pt: (pt[b * NPAGES + g * PP + j], 0))

    smem = pl.BlockSpec(memory_space=pltpu.SMEM)
    grid_spec = pltpu.PrefetchScalarGridSpec(
        num_scalar_prefetch=1,
        grid=(DB, n_steps),
        in_specs=([smem, pl.BlockSpec((1, 8, D), lambda b, g, pt: (b, 0, 0))]
                  + [page_spec(j) for j in range(PP)]
                  + [page_spec(j) for j in range(PP)]
                  + [pl.BlockSpec((1, HD, PP * PAGE), lambda b, g, pt: (g, 0, 0)),
                     pl.BlockSpec((1, 16, D), lambda b, g, pt: (b, 0, 0)),
                     pl.BlockSpec((1, 16, D), lambda b, g, pt: (b, 0, 0)),
                     pl.BlockSpec((HD, 16), lambda b, g, pt: (0, 0)),
                     pl.BlockSpec((1, HD), lambda b, g, pt: (0, 0))]),
        out_specs=pl.BlockSpec((1, 8, D), lambda b, g, pt: (b, 0, 0)),
        scratch_shapes=[pltpu.VMEM((HD, HD), F32), pltpu.VMEM((HD, 2 * HD), F32),
                        pltpu.VMEM((HD, D), BF16)],
    )
    return pl.pallas_call(
        _attn_decode_kernel,
        grid_spec=grid_spec,
        out_shape=jax.ShapeDtypeStruct((DB, 8, D), F32),
        compiler_params=_cparams(("parallel", "arbitrary")),
        name="attn_decode",
    )(page_table.reshape(-1), lam, q8, *([cache_kt] * PP), *([cache_v2] * PP),
      bias_pages, k_new, v_new, bias_new, sw.reshape(1, HD))


def _router_kernel(x_ref, wn_ref, wr_ref, br_ref, h_ref, idx_ref, gate_ref):
    h = _rms(x_ref[...], wn_ref[...])
    h_ref[...] = h
    h_hi = h.astype(BF16)
    h_lo = (h - h_hi.astype(F32)).astype(BF16)
    logits = (jnp.dot(h_hi, wr_ref[0], preferred_element_type=F32)
              + jnp.dot(h_lo, wr_ref[0], preferred_element_type=F32)
              + jnp.dot(h_hi, wr_ref[1], preferred_element_type=F32)) + br_ref[...]
    lane = lax.broadcasted_iota(jnp.int32, logits.shape, 1)
    ninf = jnp.float32(-jnp.inf)
    logits = jnp.where(lane < NEXP, logits, ninf)
    v1 = jnp.max(logits, axis=-1, keepdims=True)
    i1 = jnp.min(jnp.where(logits == v1, lane, HD), axis=-1, keepdims=True)
    rest = jnp.where(lane == i1, ninf, logits)
    v2 = jnp.max(rest, axis=-1, keepdims=True)
    i2 = jnp.min(jnp.where(rest == v2, lane, HD), axis=-1, keepdims=True)
    e = jnp.exp(v2 - v1)
    g1 = 1.0 / (1.0 + e)
    g2 = e / (1.0 + e)
    idx_ref[...] = jnp.where(lane == 0, i1, jnp.where(lane == 1, i2, 0))
    gate_ref[...] = jnp.where(lane == 0, g1, jnp.where(lane == 1, g2, 0.0))


def moe_router(x, wn, w_router, b_router):
    rows = x.shape[0]
    wr = jnp.zeros((D, HD), F32).at[:, :NEXP].set(w_router.astype(F32))
    wr_hi = wr.astype(BF16)
    wr_lo = (wr - wr_hi.astype(F32)).astype(BF16)
    br = jnp.zeros((1, HD), F32).at[0, :NEXP].set(b_router.astype(F32))
    return pl.pallas_call(
        _router_kernel,
        grid=(rows // TM,),
        in_specs=[pl.BlockSpec((TM, D), lambda i: (i, 0)),
                  pl.BlockSpec((1, D), lambda i: (0, 0)),
                  pl.BlockSpec((2, D, HD), lambda i: (0, 0, 0)),
                  pl.BlockSpec((1, HD), lambda i: (0, 0))],
        out_specs=[pl.BlockSpec((TM, D), lambda i: (i, 0)),
                   pl.BlockSpec((TM, HD), lambda i: (i, 0)),
                   pl.BlockSpec((TM, HD), lambda i: (i, 0))],
        out_shape=[jax.ShapeDtypeStruct((rows, D), F32),
                   jax.ShapeDtypeStruct((rows, HD), jnp.int32),
                   jax.ShapeDtypeStruct((rows, HD), F32)],
        compiler_params=_cparams(("parallel",)),
        name="moe_router",
    )(x, wn.reshape(1, D), jnp.stack([wr_hi, wr_lo]), br)


def _moe_kernel(te_ref, nv_ref, sa_ref, h_hbm, wa_ref, wb_ref, wd_ref, out_hbm,
                hbuf, hb16, obuf, gsem, ssem):
    del te_ref
    i = pl.program_id(0)
    j = pl.program_id(1)
    nt = pl.num_programs(0)
    last_j = pl.num_programs(1) - 1
    slot = lax.rem(i, 2)
    live = nv_ref[i] > 0

    def start_gather(tile, s):
        def body(r, carry):
            a = sa_ref[tile * MOE_TM + r]
            row = jnp.minimum(jnp.where(a < R, a, a - R), R - 1)
            pltpu.make_async_copy(h_hbm.at[pl.ds(row, 1)], hbuf.at[s, pl.ds(r, 1)], gsem.at[s]).start()
            return carry
        lax.fori_loop(0, MOE_TM, body, 0, unroll=8)

    def wait_gather(s):
        pltpu.make_async_copy(h_hbm.at[pl.ds(0, MOE_TM)], hbuf.at[s], gsem.at[s]).wait()

    def start_scatter(tile, s):
        def body(r, carry):
            a = sa_ref[tile * MOE_TM + r]
            pltpu.make_async_copy(obuf.at[s, pl.ds(r, 1)], out_hbm.at[pl.ds(a, 1)], ssem.at[s]).start()
            return carry
        lax.fori_loop(0, MOE_TM, body, 0, unroll=8)

    def wait_scatter(s):
        pltpu.make_async_copy(obuf.at[s], out_hbm.at[pl.ds(0, MOE_TM)], ssem.at[s]).wait()

    @pl.when(j == 0)
    def _():
        @pl.when((i >= 2) & (nv_ref[jnp.maximum(i - 2, 0)] > 0))
        def _():
            wait_scatter(slot)

        @pl.when(live)
        def _():
            @pl.when(i == 0)
            def _():
                start_gather(0, 0)
                obuf[1] = jnp.zeros((MOE_TM, D), F32)
                for blk in range(2):
                    pltpu.make_async_copy(obuf.at[1], out_hbm.at[pl.ds(2 * R + blk * MOE_TM, MOE_TM)],
                                          ssem.at[1]).start()
                for blk in range(2):
                    wait_scatter(1)

            nxt = jnp.minimum(i + 1, nt - 1)

            @pl.when((i + 1 < nt) & (nv_ref[nxt] > 0))
            def _():
                start_gather(nxt, 1 - slot)

            wait_gather(slot)
            hb16[...] = hbuf[slot].astype(BF16)

    @pl.when(live)
    def _():
        h = hb16[...]
        a = jnp.dot(h, wa_ref[0], preferred_element_type=F32)
        b = jnp.dot(h, wb_ref[0], preferred_element_type=F32)
        act = (a * jax.nn.sigmoid(a) * b).astype(BF16)
        y = jnp.dot(act, wd_ref[0], preferred_element_type=F32)

        @pl.when(j == 0)
        def _():
            obuf[slot] = y

        @pl.when(j > 0)
        def _():
            obuf[slot] += y

        @pl.when(j == last_j)
        def _():
            start_scatter(i, slot)

    @pl.when((i == nt - 1) & (j == last_j))
    def _():
        @pl.when(nv_ref[nt - 2] > 0)
        def _():
            wait_scatter(1 - slot)

        @pl.when(live)
        def _():
            wait_scatter(slot)


def moe_experts(tile_expert, tile_rows, slot_assign, h, w_up, w_down):
    nf = DFFE // MOE_FC
    grid_spec = pltpu.PrefetchScalarGridSpec(
        num_scalar_prefetch=3,
        grid=(MOE_NT, nf),
        in_specs=[pl.BlockSpec(memory_space=pl.ANY),
                  pl.BlockSpec((1, D, MOE_FC), lambda i, j, te, nv, sa: (te[i], 0, j)),
                  pl.BlockSpec((1, D, MOE_FC), lambda i, j, te, nv, sa: (te[i], 0, nf + j)),
                  pl.BlockSpec((1, MOE_FC, D), lambda i, j, te, nv, sa: (te[i], j, 0))],
        out_specs=pl.BlockSpec(memory_space=pl.ANY),
        scratch_shapes=[pltpu.VMEM((2, MOE_TM, D), F32), pltpu.VMEM((MOE_TM, D), BF16),
                        pltpu.VMEM((2, MOE_TM, D), F32),
                        pltpu.SemaphoreType.DMA((2,)), pltpu.SemaphoreType.DMA((2,))],
    )
    return pl.pallas_call(
        _moe_kernel,
        grid_spec=grid_spec,
        out_shape=jax.ShapeDtypeStruct((2 * R + 2 * MOE_TM, D), F32),
        compiler_params=_cparams(("arbitrary", "arbitrary")),
        name="moe_experts",
    )(tile_expert, tile_rows, slot_assign, h, w_up, w_up, w_down)


def _combine_kernel(x_ref, g_ref, e0_ref, e1_ref, wn_ref, y_ref):
    moe = g_ref[:, 0:1] * e0_ref[...] + g_ref[:, 1:2] * e1_ref[...]
    y_ref[...] = _rms(x_ref[...] + moe, wn_ref[...])


def moe_combine_norm(x, gates, eo, wn):
    rows = x.shape[0]
    nb = rows // TM
    return pl.pallas_call(
        _combine_kernel,
        grid=(rows // TM,),
        in_specs=[pl.BlockSpec((TM, D), lambda i: (i, 0)),
                  pl.BlockSpec((TM, HD), lambda i: (i, 0)),
                  pl.BlockSpec((TM, D), lambda i: (i, 0)),
                  pl.BlockSpec((TM, D), lambda i: (nb + i, 0)),
                  pl.BlockSpec((1, D), lambda i: (0, 0))],
        out_specs=pl.BlockSpec((TM, D), lambda i: (i, 0)),
        out_shape=jax.ShapeDtypeStruct((rows, D), F32),
        compiler_params=_cparams(("parallel",)),
        name="moe_combine_norm",
    )(x, gates, eo, eo, wn.reshape(1, D))


def _moe_plan(idx):
    e_flat = idx.reshape(-1)
    onehot = (e_flat[:, None] == jnp.arange(NEXP, dtype=jnp.int32)[None, :]).astype(jnp.int32)
    cum = jnp.cumsum(onehot, axis=0)
    counts = cum[-1]
    padded = ((counts + MOE_TM - 1) // MOE_TM) * MOE_TM
    ends = jnp.cumsum(padded)
    starts = ends - padded
    slot = jnp.sum(onehot * (starts[None, :] + cum - onehot), axis=1)
    pos = jnp.arange(MOE_NT * MOE_TM, dtype=jnp.int32)
    spare = 2 * R + ((pos // MOE_TM) % 2) * MOE_TM + pos % MOE_TM
    flat = jnp.arange(2 * R, dtype=jnp.int32)
    slot_assign = spare.at[slot].set((flat % 2) * R + flat // 2)
    tile_start = jnp.arange(MOE_NT, dtype=jnp.int32) * MOE_TM
    owner = (tile_start[:, None] >= starts[None, :]) & (tile_start[:, None] < ends[None, :])
    tile_rows = jnp.sum(jnp.where(owner, jnp.clip(counts[None, :] - (tile_start[:, None] - starts[None, :]),
                                                  0, MOE_TM), 0), axis=1).astype(jnp.int32)
    last_expert = jnp.max(jnp.where(counts > 0, jnp.arange(NEXP, dtype=jnp.int32), 0))
    tile_expert = jnp.where(jnp.any(owner, axis=1), jnp.argmax(owner, axis=1), last_expert).astype(jnp.int32)
    return tile_expert, tile_rows, slot_assign


def kernel(x_prompt, x_sample, state_hgrn, cache_k, cache_v, page_table, meta_tokens,
           norm_mix_w, norm_ffn_w, hg_w_in, hg_lower_bound, hg_norm_w, hg_w_out,
           at_w_qkv, at_lambda_q1, at_lambda_k1, at_lambda_q2, at_lambda_k2, at_subln_w,
           at_w_out, rel_bias_table, ff_w_up, ff_w_down, moe_w_router, moe_b_router,
           moe_w_up, moe_w_down, final_norm_w):
    xp = jnp.concatenate([jnp.broadcast_to(meta_tokens.astype(F32), (B, NMETA, D)), x_prompt,
                          jnp.zeros((B, LP - LREAL, D), F32)], axis=1)
    x0 = jnp.concatenate([xp.reshape(RP, D), x_sample.reshape(RS, D)], axis=0)
    lower = jnp.cumsum(jax.nn.softmax(hg_lower_bound.astype(F32), axis=0), axis=0)

    proj = norm_matmul(x0, norm_mix_w[0], hg_w_in[0].astype(BF16), 1024)
    y_p, st_p = hgrn_mixer(proj, lower[0], hg_norm_w[0], jnp.zeros((1, H, HD, HD), F32),
                           B, HG_C, LP // HG_C, LREAL, False)
    proj_s = jnp.pad(proj[RP:].reshape(DB, DT, 4 * D), ((0, 0), (0, HG_CS - DT), (0, 0)))
    y_s, st_s = hgrn_mixer(proj_s.reshape(DB * HG_CS, 4 * D), lower[0], hg_norm_w[0], state_hgrn[0],
                           DB, HG_CS, 1, DT, True)
    y0 = jnp.concatenate([y_p, y_s.reshape(DB, HG_CS, D)[:, :DT].reshape(RS, D)], axis=0)
    x1 = matmul_residual(y0, hg_w_out[0].astype(BF16), x0)
    x2 = dense_ffn(x1, norm_ffn_w[0], ff_w_up[0].astype(BF16), ff_w_down[0].astype(BF16))

    f = F32
    lam = (jnp.exp(jnp.sum(at_lambda_q1[0].astype(f) * at_lambda_k1[0].astype(f)))
           - jnp.exp(jnp.sum(at_lambda_q2[0].astype(f) * at_lambda_k2[0].astype(f))) + LAMBDA_INIT)
    lam = lam.reshape(1).astype(F32)
    qkv = norm_matmul(x2, norm_mix_w[1], at_w_qkv[0].astype(BF16), 1024)
    q_all, k_all, v_all = qkv[:, :D], qkv[:, D:2 * D], qkv[:, 2 * D:]
    qscale = DQK ** -0.5

    def heads_major(t, scale=1.0):
        t = (t[:RP] * scale).reshape(B, LP, H, HD).transpose(0, 2, 1, 3).astype(BF16)
        return jnp.pad(t, ((0, 0), (0, 0), (0, LA - LP), (0, 0)))

    T = AT_T
    dvals = np.arange(-(T - 1), T)
    f_diag = jnp.where(jnp.asarray(dvals >= 0)[None, :], _bias_by_distance(dvals, rel_bias_table), NEG)
    f_prev = _bias_by_distance(dvals + T, rel_bias_table)
    cfar = rel_bias_table[NBUCKETS - 1].astype(F32)
    strips = jnp.stack([_toeplitz(f_diag, T), _toeplitz(f_prev, T),
                        jnp.broadcast_to(cfar[:, None, None], (H, T, T))], axis=1)
    o_p = attn_prompt(heads_major(q_all, qscale), heads_major(k_all), heads_major(v_all),
                      strips, lam, at_subln_w[0])

    q8 = jnp.pad((q_all[RP:] * qscale).reshape(DB, DT, D), ((0, 0), (0, 8 - DT), (0, 0)))
    tq = np.minimum(np.arange(8), DT - 1)
    kpos = np.arange(PAST).reshape(NPAGES, 1, PAGE)
    bp = _bias_by_distance(PAST + tq[None, :, None] - kpos, rel_bias_table)
    bias_pages = jnp.broadcast_to(bp.transpose(1, 0, 2, 3)[:, :, None], (NPAGES, H, 2, 8, PAGE))
    bias_pages = bias_pages.reshape(NPAGES // PP, PP, HD, PAGE).transpose(0, 2, 1, 3)
    bias_pages = bias_pages.reshape(NPAGES // PP, HD, PP * PAGE)
    sn = np.arange(16)
    dn = tq[:, None] - sn[None, :]
    bn = jnp.where(jnp.asarray((dn >= 0) & (sn[None, :] < DT))[None],
                   _bias_by_distance(dn, rel_bias_table), NEG)
    bias_new = jnp.broadcast_to(bn[:, None], (H, 2, 8, 16)).reshape(HD, 16)
    k_new = jnp.pad(k_all[RP:].reshape(DB, DT, D), ((0, 0), (0, 16 - DT), (0, 0)))
    v_new = jnp.pad(v_all[RP:].reshape(DB, DT, D), ((0, 0), (0, 16 - DT), (0, 0)))
    n_phys = cache_k.shape[1]
    cache_kt = jnp.transpose(cache_k[0], (0, 2, 3, 4, 1)).reshape(n_phys * D, PAGE)
    cache_v2 = cache_v[0].reshape(n_phys * PAGE * H, HD)
    o_s = attn_decode(page_table, lam, q8, cache_kt, cache_v2, bias_pages, k_new, v_new, bias_new,
                      at_subln_w[0])
    o1 = jnp.concatenate([o_p.reshape(B, LA, D)[:, :LP].reshape(RP, D),
                          o_s[:, :DT].reshape(RS, D)], axis=0)
    x3 = matmul_residual(o1, at_w_out[0].astype(BF16), x2)

    h_moe, idx, gates = moe_router(x3, norm_ffn_w[1], moe_w_router[0], moe_b_router[0])
    tile_expert, tile_rows, slot_assign = _moe_plan(idx[:, :TOPK])
    eo = moe_experts(tile_expert, tile_rows, slot_assign, h_moe,
                     moe_w_up[0].astype(BF16), moe_w_down[0].astype(BF16))
    y = moe_combine_norm(x3, gates, eo, final_norm_w)

    y_prompt = y[:RP].reshape(B, LP, D)[:, NMETA:LREAL]
    y_sample = y[RP:].reshape(DB, DT, D)
    k_p = k_all[:RP].reshape(B, LP, H, 2, DQK)[:, :LREAL]
    v_p = v_all[:RP].reshape(B, LP, H, HD)[:, :LREAL]
    k_s = k_all[RP:].reshape(DB, DT, H, 2, DQK)
    v_s = v_all[RP:].reshape(DB, DT, H, HD)
    return (y_prompt, y_sample, st_p[None], st_s[None], k_p[None], v_p[None], k_s[None], v_s[None])
```

```python
import functools
import math

import numpy as np
import jax
import jax.numpy as jnp
from jax import lax
from jax.experimental import pallas as pl
from jax.experimental.pallas import tpu as pltpu

F32 = jnp.float32
BF16 = jnp.bfloat16

D = 1024
B = 8
SEQ = 2048
DB = 128
DT = 4
PAST = 2048
PAGE = 128
NPAGES = PAST // PAGE
NMETA = 16
LREAL = NMETA + SEQ
LP = 2176
RP = B * LP
RS = DB * DT
R = RP + RS
H = 8
HD = 128
DQK = 64
NBUCKETS = 32
MAXDIST = 128
DFF = 2816
NEXP = 8
TOPK = 2
DFFE = 3584
EPS = 1e-6
NEG = -1e30
LAMBDA_INIT = 0.8 - 0.6 * math.exp(-0.3 * 1)

VMEM_LIMIT = 56 * 1024 * 1024

TM = 512
HG_C = 128
HG_CS = 16
HG_NS = 4
AT_T = 256
MOE_TM = 512
MOE_NT = (2 * R) // MOE_TM + NEXP
MOE_FC = 1792
FFN_FC = 1408
PP = 8


def _cparams(sem, vmem=VMEM_LIMIT):
    return pltpu.CompilerParams(dimension_semantics=sem, vmem_limit_bytes=vmem)


def _rms(x, w):
    return (x * lax.rsqrt(jnp.mean(x * x, axis=-1, keepdims=True) + EPS)) * w


def _norm_matmul_kernel(x_ref, wn_ref, w_ref, o_ref, h_ref):
    @pl.when(pl.program_id(1) == 0)
    def _():
        h_ref[...] = _rms(x_ref[...], wn_ref[...]).astype(BF16)

    o_ref[...] = jnp.dot(h_ref[...], w_ref[...], preferred_element_type=F32)


def norm_matmul(x, wn, w, tn):
    rows, n = x.shape[0], w.shape[1]
    return pl.pallas_call(
        _norm_matmul_kernel,
        grid=(rows // TM, n // tn),
        in_specs=[pl.BlockSpec((TM, D), lambda i, j: (i, 0)),
                  pl.BlockSpec((1, D), lambda i, j: (0, 0)),
                  pl.BlockSpec((D, tn), lambda i, j: (0, j))],
        out_specs=pl.BlockSpec((TM, tn), lambda i, j: (i, j)),
        out_shape=jax.ShapeDtypeStruct((rows, n), F32),
        scratch_shapes=[pltpu.VMEM((TM, D), BF16)],
        compiler_params=_cparams(("parallel", "arbitrary")),
        name="norm_matmul",
    )(x, wn.reshape(1, D), w)


def _matmul_residual_kernel(ap_ref, as_ref, w_ref, x_ref, o_ref, *, n_prompt_tiles):
    def emit(a_ref):
        o_ref[...] = x_ref[...] + jnp.dot(a_ref[...].astype(BF16), w_ref[...],
                                          preferred_element_type=F32)

    i = pl.program_id(0)
    pl.when(i < n_prompt_tiles)(lambda: emit(ap_ref))
    pl.when(i >= n_prompt_tiles)(lambda: emit(as_ref))


def matmul_residual(a_prompt, a_decode, w, x):
    rows = x.shape[0]
    n_p = a_prompt.shape[0] // TM
    return pl.pallas_call(
        functools.partial(_matmul_residual_kernel, n_prompt_tiles=n_p),
        grid=(rows // TM,),
        in_specs=[pl.BlockSpec((TM, D), lambda i: (jnp.minimum(i, n_p - 1), 0)),
                  pl.BlockSpec((TM, D), lambda i: (jnp.maximum(i - n_p, 0), 0)),
                  pl.BlockSpec((D, D), lambda i: (0, 0)),
                  pl.BlockSpec((TM, D), lambda i: (i, 0))],
        out_specs=pl.BlockSpec((TM, D), lambda i: (i, 0)),
        out_shape=jax.ShapeDtypeStruct((rows, D), F32),
        compiler_params=_cparams(("parallel",)),
        name="matmul_residual",
    )(a_prompt, a_decode, w, x)


def _ffn_kernel(x_ref, wn_ref, wa_ref, wb_ref, wd_ref, o_ref, h_ref):
    j = pl.program_id(1)

    @pl.when(j == 0)
    def _():
        h_ref[...] = _rms(x_ref[...], wn_ref[...]).astype(BF16)

    h = h_ref[...]
    a = jnp.dot(h, wa_ref[...], preferred_element_type=F32)
    b = jnp.dot(h, wb_ref[...], preferred_element_type=F32)
    act = (a * jax.nn.sigmoid(a) * b).astype(BF16)
    y = jnp.dot(act, wd_ref[...], preferred_element_type=F32)

    @pl.when(j == 0)
    def _():
        o_ref[...] = x_ref[...] + y

    @pl.when(j > 0)
    def _():
        o_ref[...] += y


def dense_ffn(x, wn, w_up, w_down):
    rows = x.shape[0]
    nf = DFF // FFN_FC
    return pl.pallas_call(
        _ffn_kernel,
        grid=(rows // TM, nf),
        in_specs=[pl.BlockSpec((TM, D), lambda i, j: (i, 0)),
                  pl.BlockSpec((1, D), lambda i, j: (0, 0)),
                  pl.BlockSpec((D, FFN_FC), lambda i, j: (0, j)),
                  pl.BlockSpec((D, FFN_FC), lambda i, j: (0, nf + j)),
                  pl.BlockSpec((FFN_FC, D), lambda i, j: (j, 0))],
        out_specs=pl.BlockSpec((TM, D), lambda i, j: (i, 0)),
        out_shape=jax.ShapeDtypeStruct((rows, D), F32),
        scratch_shapes=[pltpu.VMEM((TM, D), BF16)],
        compiler_params=_cparams(("parallel", "arbitrary")),
        name="dense_ffn",
    )(x, wn.reshape(1, D), w_up, w_up, w_down)


def _hgrn_masks(C):
    nlev = int(math.log2(C))
    t = np.arange(C)[:, None]
    s = np.arange(C)[None, :]
    m = [(t == s)]
    for lev in range(nlev):
        b = 1 << lev
        m.append((t // (2 * b) == s // (2 * b)) & ((t // b) % 2 == 1) & ((s // b) % 2 == 0))
    return np.stack(m).astype(np.float32)


def _hgrn_kernel(proj_ref, lb_ref, nw_ref, msk_ref, s0_ref, y_ref, s_ref, *, C, n_valid, n_seq):
    for bb in range(n_seq):
        _hgrn_chunk(proj_ref.at[pl.ds(bb * C, C)], lb_ref, nw_ref, msk_ref, s0_ref.at[pl.ds(bb, 1)],
                    y_ref.at[pl.ds(bb * C, C)], s_ref.at[pl.ds(bb, 1)], C, n_valid)


def _hgrn_chunk(proj_ref, lb_ref, nw_ref, msk_ref, s0_ref, y_ref, s_ref, C, n_valid):
    nlev = int(math.log2(C))
    c = pl.program_id(1)

    @pl.when(c == 0)
    def _():
        s_ref[...] = s0_ref[...]

    rloc = lax.broadcasted_iota(jnp.int32, (C, HD), 0)
    valid = (c * C + rloc) < n_valid
    nt = (((1,), (1,)), ((), ()))
    tn = (((0,), (0,)), ((), ()))
    o_heads = []
    for h in range(H):
        col = slice(h * HD, (h + 1) * HD)
        lb = lb_ref[:, col]
        q = proj_ref[:, h * HD:(h + 1) * HD]
        f = proj_ref[:, D + h * HD:D + (h + 1) * HD]
        v = proj_ref[:, 2 * D + h * HD:2 * D + (h + 1) * HD]
        qs = q * jax.nn.sigmoid(q)
        fg = lb + (1.0 - lb) * jax.nn.sigmoid(f)
        k = jnp.where(valid, 1.0 - fg, 0.0)
        g = jnp.where(valid, jnp.log2(fg), 0.0)
        G = g
        sh = 1
        while sh < C:
            G = G + jnp.where(rloc >= sh, pltpu.roll(G, sh, 0), 0.0)
            sh *= 2
        vb = v.astype(BF16)
        A = msk_ref[0] * lax.dot_general(qs.astype(BF16), k.astype(BF16), nt,
                                         preferred_element_type=F32)
        for lev in range(nlev):
            b = 1 << lev
            if b >= 8:
                gm = G.reshape(C // (2 * b), 2 * b, HD)[:, b - 1:b, :]
                gmid = jnp.broadcast_to(gm, (C // (2 * b), 2 * b, HD)).reshape(C, HD)
            else:
                g8 = G.reshape(C // 8, 8, HD)
                sub = lax.broadcasted_iota(jnp.int32, (C // 8, 8, HD), 1)
                gmid = None
                for base in range(0, 8, 2 * b):
                    row = jnp.broadcast_to(g8[:, base + b - 1:base + b, :], (C // 8, 8, HD))
                    gmid = row if gmid is None else jnp.where(sub >= base, row, gmid)
                gmid = gmid.reshape(C, HD)
            right = (rloc // b) % 2 == 1
            fac = jnp.exp2(jnp.where(right, G - gmid, gmid - G))
            qt = jnp.where(right, qs * fac, 0.0).astype(BF16)
            kt = jnp.where(right, 0.0, k * fac).astype(BF16)
            A = A + msk_ref[lev + 1] * lax.dot_general(qt, kt, nt, preferred_element_type=F32)
        g_end = G[C - 1:C, :]
        qh = (qs * jnp.exp2(G)).astype(BF16)
        kh = (k * jnp.exp2(g_end - G)).astype(BF16)
        S = s_ref[0, h]
        o_h = (jnp.dot(A.astype(BF16), vb, preferred_element_type=F32)
               + jnp.dot(qh, S.astype(BF16), preferred_element_type=F32))
        e_col = jnp.broadcast_to(jnp.exp2(g_end), (HD, HD)).T
        s_ref[0, h] = e_col * S + lax.dot_general(kh, vb, tn, preferred_element_type=F32)
        o_heads.append(o_h)
    o = jnp.concatenate(o_heads, axis=1)
    gate = proj_ref[:, 3 * D:4 * D]
    y_ref[...] = _rms(o * jax.nn.sigmoid(gate), nw_ref[...]).astype(BF16)


def hgrn_mixer(proj, lb, nw, s0, nb, C, n_chunks, n_valid, s0_per_batch, n_seq=1):
    assert n_seq == 1 or (n_chunks == 1 and s0_per_batch and nb % n_seq == 0)
    msk = jnp.asarray(_hgrn_masks(C))
    nlev1 = msk.shape[0]
    s0_map = (lambda b, c: (b, 0, 0, 0)) if s0_per_batch else (lambda b, c: (0, 0, 0, 0))
    return pl.pallas_call(
        functools.partial(_hgrn_kernel, C=C, n_valid=n_valid, n_seq=n_seq),
        grid=(nb // n_seq, n_chunks),
        in_specs=[pl.BlockSpec((n_seq * C, 4 * D), lambda b, c: (b * n_chunks + c, 0)),
                  pl.BlockSpec((1, D), lambda b, c: (0, 0)),
                  pl.BlockSpec((1, D), lambda b, c: (0, 0)),
                  pl.BlockSpec((nlev1, C, C), lambda b, c: (0, 0, 0)),
                  pl.BlockSpec((n_seq, H, HD, HD), s0_map)],
        out_specs=[pl.BlockSpec((n_seq * C, D), lambda b, c: (b * n_chunks + c, 0)),
                   pl.BlockSpec((n_seq, H, HD, HD), lambda b, c: (b, 0, 0, 0))],
        out_shape=[jax.ShapeDtypeStruct((nb * n_chunks * C, D), BF16),
                   jax.ShapeDtypeStruct((nb, H, HD, HD), F32)],
        compiler_params=_cparams(("parallel", "arbitrary")),
        name="hgrn_mixer",
    )(proj, lb.reshape(1, D), nw.reshape(1, D), msk, s0)


def _bucket_of_distance(dist):
    n = np.maximum(np.asarray(dist), 0)
    max_exact = NBUCKETS // 2
    nf = np.maximum(n, 1).astype(np.float64)
    large = max_exact + (np.log(nf / max_exact) / math.log(MAXDIST / max_exact)
                         * (NBUCKETS - max_exact)).astype(np.int64)
    return np.where(n < max_exact, n, np.minimum(large, NBUCKETS - 1))


def _bias_by_distance(dist, table):
    dist = np.asarray(dist)
    onehot = np.eye(NBUCKETS, dtype=np.float32)[_bucket_of_distance(dist).reshape(-1)]
    out = jnp.einsum("nb,bh->hn", jnp.asarray(onehot), table.astype(F32),
                     precision=lax.Precision.HIGHEST)
    return out.reshape((H,) + dist.shape)


def _toeplitz(fvals, T):
    w = jnp.pad(fvals[:, ::-1], ((0, 0), (0, 1)))
    flat = jnp.tile(w, (1, T))[:, :T * (2 * T - 1)]
    return flat.reshape(H, T, 2 * T - 1)[:, :, T - 1:]


def _subln(d, w):
    return _rms(d, w) * (1.0 - LAMBDA_INIT)


def _attn_prompt_kernel(lam_ref, q_ref, k_ref, v_ref, strip_ref, sw_ref, o_ref, s_scr, m_scr, kb_scr, vx_scr):
    T = AT_T
    starts = list(range(0, LP, T))
    sizes = [min(T, LP - r0) for r0 in starts]
    nq = len(starts)
    nt = (((1,), (1,)), ((), ()))
    kb_scr[...] = k_ref[...].astype(BF16)
    vx_scr[:, :HD] = v_ref[...].astype(BF16)
    vx_scr[:, HD:] = jnp.ones((LP, HD), BF16)

    def step(t):
        slot = t % 2
        if t < nq:
            tq = sizes[t]
            qb = (q_ref[starts[t]:starts[t] + tq, :] * (DQK ** -0.5)).astype(BF16)
            lane = lax.broadcasted_iota(jnp.int32, (tq, HD), 1)
            zero = jnp.zeros_like(qb)
            qq = jnp.concatenate([jnp.where(lane < DQK, qb, zero), jnp.where(lane >= DQK, qb, zero)], axis=0)
            mx = jnp.full((2 * tq, HD), NEG, F32)
        if t > 0:
            tp = sizes[t - 1]
            m_prev = m_scr[1 - slot, :2 * tp, :]
            acc = jnp.zeros((2 * tp, 2 * HD), F32)
        for c in range(min(t + 1, nq)):
            tk = sizes[c]
            if t < nq:
                bt = strip_ref[0, min(t - c, 2), :tq, :tk]
                sc = lax.dot_general(qq, kb_scr[starts[c]:starts[c] + tk, :], nt,
                                     preferred_element_type=F32) + jnp.concatenate([bt, bt], axis=0)
                s_scr[slot, c, :2 * tq, :tk] = sc
                for u in range(tk // HD):
                    mx = jnp.maximum(mx, sc[:, u * HD:(u + 1) * HD])
            if t > 0 and c < t:
                sp = s_scr[1 - slot, c, :2 * tp, :tk]
                p = jnp.exp(jnp.concatenate([sp[:, u * HD:(u + 1) * HD] - m_prev for u in range(tk // HD)],
                                            axis=1))
                acc = acc + jnp.dot(p.astype(BF16), vx_scr[starts[c]:starts[c] + tk, :],
                                    preferred_element_type=F32)
        if t < nq:
            m_scr[slot, :2 * tq, :] = jnp.broadcast_to(jnp.max(mx, axis=-1, keepdims=True), (2 * tq, HD))
        if t > 0:
            o = acc[:, :HD] / acc[:, HD:]
            o_ref[starts[t - 1]:starts[t - 1] + tp, :] = _subln(o[:tp] - lam_ref[0] * o[tp:], sw_ref[...])

    for t in range(nq + 1):
        step(t)


def attn_prompt(qkv, strips, lam, sw):
    T = AT_T
    nq = -(-LP // T)
    smem = pl.BlockSpec(memory_space=pltpu.SMEM)
    return pl.pallas_call(
        _attn_prompt_kernel,
        grid=(B, H),
        in_specs=[smem,
                  pl.BlockSpec((LP, HD), lambda b, h: (b, h)),
                  pl.BlockSpec((LP, HD), lambda b, h: (b, H + h)),
                  pl.BlockSpec((LP, HD), lambda b, h: (b, 2 * H + h)),
                  pl.BlockSpec((1, 3, T, T), lambda b, h: (h, 0, 0, 0)),
                  pl.BlockSpec((1, HD), lambda b, h: (0, 0))],
        out_specs=pl.BlockSpec((LP, HD), lambda b, h: (b, h)),
        out_shape=jax.ShapeDtypeStruct((B * LP, D), F32),
        scratch_shapes=[pltpu.VMEM((2, nq, 2 * T, T), F32), pltpu.VMEM((2, 2 * T, HD), F32),
                        pltpu.VMEM((LP, HD), BF16), pltpu.VMEM((LP, 2 * HD), BF16)],
        compiler_params=_cparams(("parallel", "parallel")),
        name="attn_prompt",
    )(lam, qkv, qkv, qkv, strips, sw.reshape(1, HD))


def _attn_decode_kernel(pt_ref, lam_ref, q_ref, *refs):
    del pt_ref
    k_refs = refs[:PP]
    v_refs = refs[PP:2 * PP]
    bias_ref, kn_ref, vn_ref, bn_ref, sw_ref, o_ref, m_ref, acc_ref, q_scr = refs[2 * PP:]
    g = pl.program_id(1)
    last = pl.num_programs(1) - 1
    nt = (((1,), (1,)), ((), ()))

    @pl.when(g == 0)
    def _():
        m_ref[...] = jnp.full(m_ref.shape, NEG, F32)
        acc_ref[...] = jnp.zeros(acc_ref.shape, F32)
        q8 = q_ref[0]
        r = lax.broadcasted_iota(jnp.int32, (HD, D), 0)
        col = lax.broadcasted_iota(jnp.int32, (HD, D), 1)
        q_scr[...] = jnp.where(r // 8 == col // DQK, jnp.concatenate([q8] * 16, axis=0), 0.0).astype(BF16)

    qrows = q_scr[...]
    kt = jnp.concatenate([k_refs[j][...].astype(BF16) for j in range(PP)], axis=1)
    s = jnp.dot(qrows, kt, preferred_element_type=F32) + bias_ref[0]
    ones = jnp.ones((PP * PAGE, HD), BF16)

    def finish(s_new):
        smax = s[:, :PAGE]
        for j in range(1, PP):
            smax = jnp.maximum(smax, s[:, j * PAGE:(j + 1) * PAGE])
        m_cur = jnp.max(smax, axis=-1, keepdims=True)
        if s_new is not None:
            m_cur = jnp.maximum(m_cur, jnp.max(s_new, axis=-1, keepdims=True))
        m_old = m_ref[...]
        m_new = jnp.maximum(m_old, jnp.broadcast_to(m_cur, (HD, HD)))
        m_ref[...] = m_new
        alpha = jnp.exp(m_old - m_new)
        alpha2 = jnp.concatenate([alpha, alpha], axis=1)
        p = jnp.exp(s - jnp.concatenate([m_new] * PP, axis=1)).astype(BF16)
        if s_new is not None:
            p_new = jnp.exp(s_new - m_new[:, :16]).astype(BF16)
        for hh in range(H):
            rows = slice(hh * 16, (hh + 1) * 16)
            vh = jnp.concatenate([v_refs[j][pl.ds(hh, PAGE, stride=H), :].astype(BF16)
                                  for j in range(PP)], axis=0)
            pv = jnp.dot(p[rows], jnp.concatenate([vh, ones], axis=1), preferred_element_type=F32)
            if s_new is not None:
                vn = vn_ref[0, :, hh * HD:(hh + 1) * HD].astype(BF16)
                pv = pv + jnp.dot(p_new[rows], jnp.concatenate([vn, ones[:16]], axis=1),
                                  preferred_element_type=F32)
            acc_ref[rows, :] = alpha2[rows] * acc_ref[rows, :] + pv

    @pl.when(g < last)
    def _():
        finish(None)

    @pl.when(g == last)
    def _():
        finish(lax.dot_general(qrows, kn_ref[0].astype(BF16), nt,
                               preferred_element_type=F32) + bn_ref[...])
        acc = acc_ref[...]
        o = acc[:, :HD] / acc[:, HD:]
        outs = []
        for hh in range(H):
            o1 = o[hh * 16:hh * 16 + 8]
            o2 = o[hh * 16 + 8:hh * 16 + 16]
            outs.append(_subln(o1 - lam_ref[0] * o2, sw_ref[...]))
        o_ref[0] = jnp.concatenate(outs, axis=1)


def attn_decode(page_table, lam, q8, cache_kt, cache_v2, bias_pages, k_new, v_new, bias_new, sw):
    n_steps = NPAGES // PP

    def page_spec(j):
        return pl.BlockSpec((D, PAGE), lambda b, g, pt: (pt[b * NPAGES + g * PP + j], 0))

    smem = pl.BlockSpec(memory_space=pltpu.SMEM)
    grid_spec = pltpu.PrefetchScalarGridSpec(
        num_scalar_prefetch=1,
        grid=(DB, n_steps),
        in_specs=([smem, pl.BlockSpec((1, 8, D), lambda b, g, pt: (b, 0, 0))]
                  + [page_spec(j) for j in range(PP)]
                  + [page_spec(j) for j in range(PP)]
                  + [pl.BlockSpec((1, HD, PP * PAGE), lambda b, g, pt: (g, 0, 0)),
                     pl.BlockSpec((1, 16, D), lambda b, g, pt: (b, 0, 0)),
                     pl.BlockSpec((1, 16, D), lambda b, g, pt: (b, 0, 0)),
                     pl.BlockSpec((HD, 16), lambda b, g, pt: (0, 0)),
                     pl.BlockSpec((1, HD), lambda b, g, pt: (0, 0))]),
        out_specs=pl.BlockSpec((1, 8, D), lambda b, g, pt: (b, 0, 0)),
        scratch_shapes=[pltpu.VMEM((HD, HD), F32), pltpu.VMEM((HD, 2 * HD), F32),
                        pltpu.VMEM((HD, D), BF16)],
    )
    return pl.pallas_call(
        _attn_decode_kernel,
        grid_spec=grid_spec,
        out_shape=jax.ShapeDtypeStruct((DB, 8, D), F32),
        compiler_params=_cparams(("parallel", "arbitrary")),
        name="attn_decode",
    )(page_table.reshape(-1), lam, q8, *([cache_kt] * PP), *([cache_v2] * PP),
      bias_pages, k_new, v_new, bias_new, sw.reshape(1, HD))


def _router_kernel(x_ref, wn_ref, wr_ref, br_ref, h_ref, idx_ref, gate_ref):
    h = _rms(x_ref[...], wn_ref[...])
    h_ref[...] = h
    h_hi = h.astype(BF16)
    h_lo = (h - h_hi.astype(F32)).astype(BF16)
    logits = (jnp.dot(h_hi, wr_ref[0], preferred_element_type=F32)
              + jnp.dot(h_lo, wr_ref[0], preferred_element_type=F32)
              + jnp.dot(h_hi, wr_ref[1], preferred_element_type=F32)) + br_ref[...]
    lane = lax.broadcasted_iota(jnp.int32, logits.shape, 1)
    ninf = jnp.float32(-jnp.inf)
    logits = jnp.where(lane < NEXP, logits, ninf)
    v1 = jnp.max(logits, axis=-1, keepdims=True)
    i1 = jnp.min(jnp.where(logits == v1, lane, HD), axis=-1, keepdims=True)
    rest = jnp.where(lane == i1, ninf, logits)
    v2 = jnp.max(rest, axis=-1, keepdims=True)
    i2 = jnp.min(jnp.where(rest == v2, lane, HD), axis=-1, keepdims=True)
    e = jnp.exp(v2 - v1)
    g1 = 1.0 / (1.0 + e)
    g2 = e / (1.0 + e)
    idx_ref[...] = jnp.where(lane == 0, i1, jnp.where(lane == 1, i2, 0))
    gate_ref[...] = jnp.where(lane == 0, g1, jnp.where(lane == 1, g2, 0.0))


def moe_router(x, wn, w_router, b_router):
    rows = x.shape[0]
    wr = jnp.zeros((D, HD), F32).at[:, :NEXP].set(w_router.astype(F32))
    wr_hi = wr.astype(BF16)
    wr_lo = (wr - wr_hi.astype(F32)).astype(BF16)
    br = jnp.zeros((1, HD), F32).at[0, :NEXP].set(b_router.astype(F32))
    return pl.pallas_call(
        _router_kernel,
        grid=(rows // TM,),
        in_specs=[pl.BlockSpec((TM, D), lambda i: (i, 0)),
                  pl.BlockSpec((1, D), lambda i: (0, 0)),
                  pl.BlockSpec((2, D, HD), lambda i: (0, 0, 0)),
                  pl.BlockSpec((1, HD), lambda i: (0, 0))],
        out_specs=[pl.BlockSpec((TM, D), lambda i: (i, 0)),
                   pl.BlockSpec((TM, HD), lambda i: (i, 0)),
                   pl.BlockSpec((TM, HD), lambda i: (i, 0))],
        out_shape=[jax.ShapeDtypeStruct((rows, D), F32),
                   jax.ShapeDtypeStruct((rows, HD), jnp.int32),
                   jax.ShapeDtypeStruct((rows, HD), F32)],
        compiler_params=_cparams(("parallel",)),
        name="moe_router",
    )(x, wn.reshape(1, D), jnp.stack([wr_hi, wr_lo]), br)


def _moe_kernel(te_ref, nv_ref, sa_ref, h_hbm, wa_ref, wb_ref, wd_ref, out_hbm,
                hbuf, hb16, obuf, gsem, ssem):
    del te_ref
    i = pl.program_id(0)
    j = pl.program_id(1)
    nt = pl.num_programs(0)
    last_j = pl.num_programs(1) - 1
    slot = lax.rem(i, 2)
    live = nv_ref[i] > 0

    def start_gather(tile, s):
        def body(r, carry):
            a = sa_ref[tile * MOE_TM + r]
            row = jnp.minimum(jnp.where(a < R, a, a - R), R - 1)
            pltpu.make_async_copy(h_hbm.at[pl.ds(row, 1)], hbuf.at[s, pl.ds(r, 1)], gsem.at[s]).start()
            return carry
        lax.fori_loop(0, MOE_TM, body, 0, unroll=8)

    def wait_gather(s):
        pltpu.make_async_copy(h_hbm.at[pl.ds(0, MOE_TM)], hbuf.at[s], gsem.at[s]).wait()

    def start_scatter(tile, s):
        def body(r, carry):
            a = sa_ref[tile * MOE_TM + r]
            pltpu.make_async_copy(obuf.at[s, pl.ds(r, 1)], out_hbm.at[pl.ds(a, 1)], ssem.at[s]).start()
            return carry
        lax.fori_loop(0, MOE_TM, body, 0, unroll=8)

    def wait_scatter(s):
        pltpu.make_async_copy(obuf.at[s], out_hbm.at[pl.ds(0, MOE_TM)], ssem.at[s]).wait()

    @pl.when(j == 0)
    def _():
        @pl.when((i >= 2) & (nv_ref[jnp.maximum(i - 2, 0)] > 0))
        def _():
            wait_scatter(slot)

        @pl.when(live)
        def _():
            @pl.when(i == 0)
            def _():
                start_gather(0, 0)
                obuf[1] = jnp.zeros((MOE_TM, D), F32)
                for blk in range(2):
                    pltpu.make_async_copy(obuf.at[1], out_hbm.at[pl.ds(2 * R + blk * MOE_TM, MOE_TM)],
                                          ssem.at[1]).start()
                for blk in range(2):
                    wait_scatter(1)

            nxt = jnp.minimum(i + 1, nt - 1)

            @pl.when((i + 1 < nt) & (nv_ref[nxt] > 0))
            def _():
                start_gather(nxt, 1 - slot)

            wait_gather(slot)
            hb16[...] = hbuf[slot].astype(BF16)

    @pl.when(live)
    def _():
        h = hb16[...]
        a = jnp.dot(h, wa_ref[0], preferred_element_type=F32)
        b = jnp.dot(h, wb_ref[0], preferred_element_type=F32)
        act = (a * jax.nn.sigmoid(a) * b).astype(BF16)
        y = jnp.dot(act, wd_ref[0], preferred_element_type=F32)

        @pl.when(j == 0)
        def _():
            obuf[slot] = y

        @pl.when(j > 0)
        def _():
            obuf[slot] += y

        @pl.when(j == last_j)
        def _():
            start_scatter(i, slot)

    @pl.when((i == nt - 1) & (j == last_j))
    def _():
        @pl.when(nv_ref[nt - 2] > 0)
        def _():
            wait_scatter(1 - slot)

        @pl.when(live)
        def _():
            wait_scatter(slot)


def moe_experts(tile_expert, tile_rows, slot_assign, h, w_up, w_down):
    nf = DFFE // MOE_FC
    grid_spec = pltpu.PrefetchScalarGridSpec(
        num_scalar_prefetch=3,
        grid=(MOE_NT, nf),
        in_specs=[pl.BlockSpec(memory_space=pl.ANY),
                  pl.BlockSpec((1, D, MOE_FC), lambda i, j, te, nv, sa: (te[i], 0, j)),
                  pl.BlockSpec((1, D, MOE_FC), lambda i, j, te, nv, sa: (te[i], 0, nf + j)),
                  pl.BlockSpec((1, MOE_FC, D), lambda i, j, te, nv, sa: (te[i], j, 0))],
        out_specs=pl.BlockSpec(memory_space=pl.ANY),
        scratch_shapes=[pltpu.VMEM((2, MOE_TM, D), F32), pltpu.VMEM((MOE_TM, D), BF16),
                        pltpu.VMEM((2, MOE_TM, D), F32),
                        pltpu.SemaphoreType.DMA((2,)), pltpu.SemaphoreType.DMA((2,))],
    )
    return pl.pallas_call(
        _moe_kernel,
        grid_spec=grid_spec,
        out_shape=jax.ShapeDtypeStruct((2 * R + 2 * MOE_TM, D), F32),
        compiler_params=_cparams(("arbitrary", "arbitrary")),
        name="moe_experts",
    )(tile_expert, tile_rows, slot_assign, h, w_up, w_up, w_down)


def _combine_kernel(x_ref, g_ref, e0_ref, e1_ref, wn_ref, y_ref):
    moe = g_ref[:, 0:1] * e0_ref[...] + g_ref[:, 1:2] * e1_ref[...]
    y_ref[...] = _rms(x_ref[...] + moe, wn_ref[...])


def moe_combine_norm(x, gates, eo, wn):
    rows = x.shape[0]
    nb = rows // TM
    return pl.pallas_call(
        _combine_kernel,
        grid=(rows // TM,),
        in_specs=[pl.BlockSpec((TM, D), lambda i: (i, 0)),
                  pl.BlockSpec((TM, HD), lambda i: (i, 0)),
                  pl.BlockSpec((TM, D), lambda i: (i, 0)),
                  pl.BlockSpec((TM, D), lambda i: (nb + i, 0)),
                  pl.BlockSpec((1, D), lambda i: (0, 0))],
        out_specs=pl.BlockSpec((TM, D), lambda i: (i, 0)),
        out_shape=jax.ShapeDtypeStruct((rows, D), F32),
        compiler_params=_cparams(("parallel",)),
        name="moe_combine_norm",
    )(x, gates, eo, eo, wn.reshape(1, D))


def _moe_plan(idx):
    e_flat = idx.reshape(-1)
    onehot = (e_flat[:, None] == jnp.arange(NEXP, dtype=jnp.int32)[None, :]).astype(jnp.int32)
    cum = jnp.cumsum(onehot, axis=0)
    counts = cum[-1]
    padded = ((counts + MOE_TM - 1) // MOE_TM) * MOE_TM
    ends = jnp.cumsum(padded)
    starts = ends - padded
    slot = jnp.sum(onehot * (starts[None, :] + cum - onehot), axis=1)
    pos = jnp.arange(MOE_NT * MOE_TM, dtype=jnp.int32)
    spare = 2 * R + ((pos // MOE_TM) % 2) * MOE_TM + pos % MOE_TM
    flat = jnp.arange(2 * R, dtype=jnp.int32)
    slot_assign = spare.at[slot].set((flat % 2) * R + flat // 2)
    tile_start = jnp.arange(MOE_NT, dtype=jnp.int32) * MOE_TM
    owner = (tile_start[:, None] >= starts[None, :]) & (tile_start[:, None] < ends[None, :])
    tile_rows = jnp.sum(jnp.where(owner, jnp.clip(counts[None, :] - (tile_start[:, None] - starts[None, :]),
                                                  0, MOE_TM), 0), axis=1).astype(jnp.int32)
    last_expert = jnp.max(jnp.where(counts > 0, jnp.arange(NEXP, dtype=jnp.int32), 0))
    tile_expert = jnp.where(jnp.any(owner, axis=1), jnp.argmax(owner, axis=1), last_expert).astype(jnp.int32)
    return tile_expert, tile_rows, slot_assign


def kernel(x_prompt, x_sample, state_hgrn, cache_k, cache_v, page_table, meta_tokens,
           norm_mix_w, norm_ffn_w, hg_w_in, hg_lower_bound, hg_norm_w, hg_w_out,
           at_w_qkv, at_lambda_q1, at_lambda_k1, at_lambda_q2, at_lambda_k2, at_subln_w,
           at_w_out, rel_bias_table, ff_w_up, ff_w_down, moe_w_router, moe_b_router,
           moe_w_up, moe_w_down, final_norm_w):
    xp = jnp.concatenate([jnp.broadcast_to(meta_tokens.astype(F32), (B, NMETA, D)), x_prompt,
                          jnp.zeros((B, LP - LREAL, D), F32)], axis=1)
    x0 = jnp.concatenate([xp.reshape(RP, D), x_sample.reshape(RS, D)], axis=0)
    lower = jnp.cumsum(jax.nn.softmax(hg_lower_bound.astype(F32), axis=0), axis=0)

    proj = norm_matmul(x0, norm_mix_w[0], hg_w_in[0].astype(BF16), 1024)
    y_p, st_p = hgrn_mixer(proj, lower[0], hg_norm_w[0], jnp.zeros((1, H, HD, HD), F32),
                           B, HG_C, LP // HG_C, LREAL, False)
    proj_s = jnp.pad(proj[RP:].reshape(DB, DT, 4 * D), ((0, 0), (0, HG_CS - DT), (0, 0)))
    y_s, st_s = hgrn_mixer(proj_s.reshape(DB * HG_CS, 4 * D), lower[0], hg_norm_w[0], state_hgrn[0],
                           DB, HG_CS, 1, DT, True, n_seq=HG_NS)
    x1 = matmul_residual(y_p, y_s.reshape(DB, HG_CS, D)[:, :DT].reshape(RS, D),
                         hg_w_out[0].astype(BF16), x0)
    x2 = dense_ffn(x1, norm_ffn_w[0], ff_w_up[0].astype(BF16), ff_w_down[0].astype(BF16))

    f = F32
    lam = (jnp.exp(jnp.sum(at_lambda_q1[0].astype(f) * at_lambda_k1[0].astype(f)))
           - jnp.exp(jnp.sum(at_lambda_q2[0].astype(f) * at_lambda_k2[0].astype(f))) + LAMBDA_INIT)
    lam = lam.reshape(1).astype(F32)
    qkv = norm_matmul(x2, norm_mix_w[1], at_w_qkv[0].astype(BF16), 1024)
    q_all, k_all, v_all = qkv[:, :D], qkv[:, D:2 * D], qkv[:, 2 * D:]
    qscale = DQK ** -0.5

    T = AT_T
    dvals = np.arange(-(T - 1), T)
    f_diag = jnp.where(jnp.asarray(dvals >= 0)[None, :], _bias_by_distance(dvals, rel_bias_table), NEG)
    f_prev = _bias_by_distance(dvals + T, rel_bias_table)
    cfar = rel_bias_table[NBUCKETS - 1].astype(F32)
    strips = jnp.stack([_toeplitz(f_diag, T), _toeplitz(f_prev, T),
                        jnp.broadcast_to(cfar[:, None, None], (H, T, T))], axis=1)
    o_p = attn_prompt(qkv, strips, lam, at_subln_w[0])

    q8 = jnp.pad((q_all[RP:] * qscale).reshape(DB, DT, D), ((0, 0), (0, 8 - DT), (0, 0)))
    tq = np.minimum(np.arange(8), DT - 1)
    kpos = np.arange(PAST).reshape(NPAGES, 1, PAGE)
    bp = _bias_by_distance(PAST + tq[None, :, None] - kpos, rel_bias_table)
    bias_pages = jnp.broadcast_to(bp.transpose(1, 0, 2, 3)[:, :, None], (NPAGES, H, 2, 8, PAGE))
    bias_pages = bias_pages.reshape(NPAGES // PP, PP, HD, PAGE).transpose(0, 2, 1, 3)
    bias_pages = bias_pages.reshape(NPAGES // PP, HD, PP * PAGE)
    sn = np.arange(16)
    dn = tq[:, None] - sn[None, :]
    bn = jnp.where(jnp.asarray((dn >= 0) & (sn[None, :] < DT))[None],
                   _bias_by_distance(dn, rel_bias_table), NEG)
    bias_new = jnp.broadcast_to(bn[:, None], (H, 2, 8, 16)).reshape(HD, 16)
    k_new = jnp.pad(k_all[RP:].reshape(DB, DT, D), ((0, 0), (0, 16 - DT), (0, 0)))
    v_new = jnp.pad(v_all[RP:].reshape(DB, DT, D), ((0, 0), (0, 16 - DT), (0, 0)))
    n_phys = cache_k.shape[1]
    cache_kt = jnp.transpose(cache_k[0], (0, 2, 3, 4, 1)).reshape(n_phys * D, PAGE)
    cache_v2 = cache_v[0].reshape(n_phys * PAGE * H, HD)
    o_s = attn_decode(page_table, lam, q8, cache_kt, cache_v2, bias_pages, k_new, v_new, bias_new,
                      at_subln_w[0])
    x3 = matmul_residual(o_p, o_s[:, :DT].reshape(RS, D), at_w_out[0].astype(BF16), x2)

    h_moe, idx, gates = moe_router(x3, norm_ffn_w[1], moe_w_router[0], moe_b_router[0])
    tile_expert, tile_rows, slot_assign = _moe_plan(idx[:, :TOPK])
    eo = moe_experts(tile_expert, tile_rows, slot_assign, h_moe,
                     moe_w_up[0].astype(BF16), moe_w_down[0].astype(BF16))
    y = moe_combine_norm(x3, gates, eo, final_norm_w)

    y_prompt = y[:RP].reshape(B, LP, D)[:, NMETA:LREAL]
    y_sample = y[RP:].reshape(DB, DT, D)
    k_p = k_all[:RP].reshape(B, LP, H, 2, DQK)[:, :LREAL]
    v_p = v_all[:RP].reshape(B, LP, H, HD)[:, :LREAL]
    k_s = k_all[RP:].reshape(DB, DT, H, 2, DQK)
    v_s = v_all[RP:].reshape(DB, DT, H, HD)
    return (y_prompt, y_sample, st_p[None], st_s[None], k_p[None], v_p[None], k_s[None], v_s[None])
```

```python
import functools
import math

import numpy as np
import jax
import jax.numpy as jnp
from jax import lax
from jax.experimental import pallas as pl
from jax.experimental.pallas import tpu as pltpu

F32 = jnp.float32
BF16 = jnp.bfloat16

D = 1024
B = 8
SEQ = 2048
DB = 128
DT = 4
PAST = 2048
PAGE = 128
NPAGES = PAST // PAGE
NMETA = 16
LREAL = NMETA + SEQ
LP = 2176
RP = B * LP
RS = DB * DT
R = RP + RS
H = 8
HD = 128
DQK = 64
NBUCKETS = 32
MAXDIST = 128
DFF = 2816
NEXP = 8
TOPK = 2
DFFE = 3584
EPS = 1e-6
NEG = -1e30
LAMBDA_INIT = 0.8 - 0.6 * math.exp(-0.3 * 1)

VMEM_LIMIT = 56 * 1024 * 1024

TM = 512
HG_C = 128
HG_CS = 16
HG_NS = 4
AT_T = 256
MOE_TM = 512
MOE_NT = (2 * R) // MOE_TM + NEXP
MOE_FC = 1792
FFN_FC = 1408
PP = 8


def _cparams(sem, vmem=VMEM_LIMIT):
    return pltpu.CompilerParams(dimension_semantics=sem, vmem_limit_bytes=vmem)


def _rms(x, w):
    return (x * lax.rsqrt(jnp.mean(x * x, axis=-1, keepdims=True) + EPS)) * w


def _norm_matmul_kernel(x_ref, wn_ref, w_ref, o_ref, h_ref):
    @pl.when(pl.program_id(1) == 0)
    def _():
        h_ref[...] = _rms(x_ref[...], wn_ref[...]).astype(BF16)

    o_ref[...] = jnp.dot(h_ref[...], w_ref[...], preferred_element_type=F32)


def norm_matmul(x, wn, w, tn):
    rows, n = x.shape[0], w.shape[1]
    return pl.pallas_call(
        _norm_matmul_kernel,
        grid=(rows // TM, n // tn),
        in_specs=[pl.BlockSpec((TM, D), lambda i, j: (i, 0)),
                  pl.BlockSpec((1, D), lambda i, j: (0, 0)),
                  pl.BlockSpec((D, tn), lambda i, j: (0, j))],
        out_specs=pl.BlockSpec((TM, tn), lambda i, j: (i, j)),
        out_shape=jax.ShapeDtypeStruct((rows, n), F32),
        scratch_shapes=[pltpu.VMEM((TM, D), BF16)],
        compiler_params=_cparams(("parallel", "arbitrary")),
        name="norm_matmul",
    )(x, wn.reshape(1, D), w)


def _matmul_residual_kernel(ap_ref, as_ref, w_ref, x_ref, o_ref, *, n_prompt_tiles):
    def emit(a_ref):
        o_ref[...] = x_ref[...] + jnp.dot(a_ref[...].astype(BF16), w_ref[...],
                                          preferred_element_type=F32)

    i = pl.program_id(0)
    pl.when(i < n_prompt_tiles)(lambda: emit(ap_ref))
    pl.when(i >= n_prompt_tiles)(lambda: emit(as_ref))


def matmul_residual(a_prompt, a_decode, w, x):
    rows = x.shape[0]
    n_p = a_prompt.shape[0] // TM
    return pl.pallas_call(
        functools.partial(_matmul_residual_kernel, n_prompt_tiles=n_p),
        grid=(rows // TM,),
        in_specs=[pl.BlockSpec((TM, D), lambda i: (jnp.minimum(i, n_p - 1), 0)),
                  pl.BlockSpec((TM, D), lambda i: (jnp.maximum(i - n_p, 0), 0)),
                  pl.BlockSpec((D, D), lambda i: (0, 0)),
                  pl.BlockSpec((TM, D), lambda i: (i, 0))],
        out_specs=pl.BlockSpec((TM, D), lambda i: (i, 0)),
        out_shape=jax.ShapeDtypeStruct((rows, D), F32),
        compiler_params=_cparams(("parallel",)),
        name="matmul_residual",
    )(a_prompt, a_decode, w, x)


def _ffn_kernel(x_ref, wn_ref, wa_ref, wb_ref, wd_ref, o_ref, h_ref):
    j = pl.program_id(1)

    @pl.when(j == 0)
    def _():
        h_ref[...] = _rms(x_ref[...], wn_ref[...]).astype(BF16)

    h = h_ref[...]
    a = jnp.dot(h, wa_ref[...], preferred_element_type=F32)
    b = jnp.dot(h, wb_ref[...], preferred_element_type=F32)
    act = (a * jax.nn.sigmoid(a) * b).astype(BF16)
    y = jnp.dot(act, wd_ref[...], preferred_element_type=F32)

    @pl.when(j == 0)
    def _():
        o_ref[...] = x_ref[...] + y

    @pl.when(j > 0)
    def _():
        o_ref[...] += y


def dense_ffn(x, wn, w_up, w_down):
    rows = x.shape[0]
    nf = DFF // FFN_FC
    return pl.pallas_call(
        _ffn_kernel,
        grid=(rows // TM, nf),
        in_specs=[pl.BlockSpec((TM, D), lambda i, j: (i, 0)),
                  pl.BlockSpec((1, D), lambda i, j: (0, 0)),
                  pl.BlockSpec((D, FFN_FC), lambda i, j: (0, j)),
                  pl.BlockSpec((D, FFN_FC), lambda i, j: (0, nf + j)),
                  pl.BlockSpec((FFN_FC, D), lambda i, j: (j, 0))],
        out_specs=pl.BlockSpec((TM, D), lambda i, j: (i, 0)),
        out_shape=jax.ShapeDtypeStruct((rows, D), F32),
        scratch_shapes=[pltpu.VMEM((TM, D), BF16)],
        compiler_params=_cparams(("parallel", "arbitrary")),
        name="dense_ffn",
    )(x, wn.reshape(1, D), w_up, w_up, w_down)


def _hgrn_masks(C):
    nlev = int(math.log2(C))
    t = np.arange(C)[:, None]
    s = np.arange(C)[None, :]
    m = [(t == s)]
    for lev in range(nlev):
        b = 1 << lev
        m.append((t // (2 * b) == s // (2 * b)) & ((t // b) % 2 == 1) & ((s // b) % 2 == 0))
    return np.stack(m).astype(np.float32)


def _hgrn_kernel(proj_ref, lb_ref, nw_ref, msk_ref, s0_ref, y_ref, s_ref, *, C, n_valid, n_seq):
    for bb in range(n_seq):
        _hgrn_chunk(proj_ref.at[pl.ds(bb * C, C)], lb_ref, nw_ref, msk_ref, s0_ref.at[pl.ds(bb, 1)],
                    y_ref.at[pl.ds(bb * C, C)], s_ref.at[pl.ds(bb, 1)], C, n_valid)


def _hgrn_chunk(proj_ref, lb_ref, nw_ref, msk_ref, s0_ref, y_ref, s_ref, C, n_valid):
    nlev = int(math.log2(C))
    c = pl.program_id(1)

    @pl.when(c == 0)
    def _():
        s_ref[...] = s0_ref[...]

    rloc = lax.broadcasted_iota(jnp.int32, (C, HD), 0)
    valid = (c * C + rloc) < n_valid
    nt = (((1,), (1,)), ((), ()))
    tn = (((0,), (0,)), ((), ()))
    o_heads = []
    for h in range(H):
        col = slice(h * HD, (h + 1) * HD)
        lb = lb_ref[:, col]
        q = proj_ref[:, h * HD:(h + 1) * HD]
        f = proj_ref[:, D + h * HD:D + (h + 1) * HD]
        v = proj_ref[:, 2 * D + h * HD:2 * D + (h + 1) * HD]
        qs = q * jax.nn.sigmoid(q)
        fg = lb + (1.0 - lb) * jax.nn.sigmoid(f)
        k = jnp.where(valid, 1.0 - fg, 0.0)
        g = jnp.where(valid, jnp.log2(fg), 0.0)
        G = g
        sh = 1
        while sh < C:
            G = G + jnp.where(rloc >= sh, pltpu.roll(G, sh, 0), 0.0)
            sh *= 2
        vb = v.astype(BF16)
        A = msk_ref[0] * lax.dot_general(qs.astype(BF16), k.astype(BF16), nt,
                                         preferred_element_type=F32)
        for lev in range(nlev):
            b = 1 << lev
            if b >= 8:
                gm = G.reshape(C // (2 * b), 2 * b, HD)[:, b - 1:b, :]
                gmid = jnp.broadcast_to(gm, (C // (2 * b), 2 * b, HD)).reshape(C, HD)
            else:
                g8 = G.reshape(C // 8, 8, HD)
                sub = lax.broadcasted_iota(jnp.int32, (C // 8, 8, HD), 1)
                gmid = None
                for base in range(0, 8, 2 * b):
                    row = jnp.broadcast_to(g8[:, base + b - 1:base + b, :], (C // 8, 8, HD))
                    gmid = row if gmid is None else jnp.where(sub >= base, row, gmid)
                gmid = gmid.reshape(C, HD)
            right = (rloc // b) % 2 == 1
            fac = jnp.exp2(jnp.where(right, G - gmid, gmid - G))
            qt = jnp.where(right, qs * fac, 0.0).astype(BF16)
            kt = jnp.where(right, 0.0, k * fac).astype(BF16)
            A = A + msk_ref[lev + 1] * lax.dot_general(qt, kt, nt, preferred_element_type=F32)
        g_end = G[C - 1:C, :]
        qh = (qs * jnp.exp2(G)).astype(BF16)
        kh = (k * jnp.exp2(g_end - G)).astype(BF16)
        S = s_ref[0, h]
        o_h = (jnp.dot(A.astype(BF16), vb, preferred_element_type=F32)
               + jnp.dot(qh, S.astype(BF16), preferred_element_type=F32))
        e_col = jnp.broadcast_to(jnp.exp2(g_end), (HD, HD)).T
        s_ref[0, h] = e_col * S + lax.dot_general(kh, vb, tn, preferred_element_type=F32)
        o_heads.append(o_h)
    o = jnp.concatenate(o_heads, axis=1)
    gate = proj_ref[:, 3 * D:4 * D]
    y_ref[...] = _rms(o * jax.nn.sigmoid(gate), nw_ref[...]).astype(BF16)


def _hgrn_fused_kernel(x_ref, xn_ref, wn_ref, w_ref, lb_ref, nw_ref, msk_ref, s0_ref, y_ref, s_ref,
                       proj_scr, *, C, n_valid):
    c = pl.program_id(1)
    slot = lax.rem(c, 2)

    def project(x):
        return jnp.dot(_rms(x, wn_ref[...]).astype(BF16), w_ref[...], preferred_element_type=F32)

    @pl.when(c == 0)
    def _():
        proj_scr[0] = project(x_ref[...])

    proj_scr[1 - slot] = project(xn_ref[...])
    _hgrn_chunk(proj_scr.at[slot], lb_ref, nw_ref, msk_ref, s0_ref, y_ref, s_ref, C, n_valid)


def hgrn_mixer_fused(x, wn, w_in, lb, nw, nb, C, n_chunks, n_valid):
    msk = jnp.asarray(_hgrn_masks(C))
    nlev1 = msk.shape[0]
    const2 = lambda b, c: (0, 0)
    return pl.pallas_call(
        functools.partial(_hgrn_fused_kernel, C=C, n_valid=n_valid),
        grid=(nb, n_chunks),
        in_specs=[pl.BlockSpec((C, D), lambda b, c: (b * n_chunks + c, 0)),
                  pl.BlockSpec((C, D), lambda b, c: (b * n_chunks + jnp.minimum(c + 1, n_chunks - 1), 0)),
                  pl.BlockSpec((1, D), const2),
                  pl.BlockSpec((D, 4 * D), const2),
                  pl.BlockSpec((1, D), const2),
                  pl.BlockSpec((1, D), const2),
                  pl.BlockSpec((nlev1, C, C), lambda b, c: (0, 0, 0)),
                  pl.BlockSpec((1, H, HD, HD), lambda b, c: (0, 0, 0, 0))],
        out_specs=[pl.BlockSpec((C, D), lambda b, c: (b * n_chunks + c, 0)),
                   pl.BlockSpec((1, H, HD, HD), lambda b, c: (b, 0, 0, 0))],
        out_shape=[jax.ShapeDtypeStruct((nb * n_chunks * C, D), BF16),
                   jax.ShapeDtypeStruct((nb, H, HD, HD), F32)],
        scratch_shapes=[pltpu.VMEM((2, C, 4 * D), F32)],
        compiler_params=_cparams(("parallel", "arbitrary")),
        name="hgrn_mixer_fused",
    )(x, x, wn.reshape(1, D), w_in, lb.reshape(1, D), nw.reshape(1, D), msk,
      jnp.zeros((1, H, HD, HD), F32))


def hgrn_mixer(proj, lb, nw, s0, nb, C, n_chunks, n_valid, s0_per_batch, n_seq=1):
    assert n_seq == 1 or (n_chunks == 1 and s0_per_batch and nb % n_seq == 0)
    msk = jnp.asarray(_hgrn_masks(C))
    nlev1 = msk.shape[0]
    s0_map = (lambda b, c: (b, 0, 0, 0)) if s0_per_batch else (lambda b, c: (0, 0, 0, 0))
    return pl.pallas_call(
        functools.partial(_hgrn_kernel, C=C, n_valid=n_valid, n_seq=n_seq),
        grid=(nb // n_seq, n_chunks),
        in_specs=[pl.BlockSpec((n_seq * C, 4 * D), lambda b, c: (b * n_chunks + c, 0)),
                  pl.BlockSpec((1, D), lambda b, c: (0, 0)),
                  pl.BlockSpec((1, D), lambda b, c: (0, 0)),
                  pl.BlockSpec((nlev1, C, C), lambda b, c: (0, 0, 0)),
                  pl.BlockSpec((n_seq, H, HD, HD), s0_map)],
        out_specs=[pl.BlockSpec((n_seq * C, D), lambda b, c: (b * n_chunks + c, 0)),
                   pl.BlockSpec((n_seq, H, HD, HD), lambda b, c: (b, 0, 0, 0))],
        out_shape=[jax.ShapeDtypeStruct((nb * n_chunks * C, D), BF16),
                   jax.ShapeDtypeStruct((nb, H, HD, HD), F32)],
        compiler_params=_cparams(("parallel", "arbitrary")),
        name="hgrn_mixer",
    )(proj, lb.reshape(1, D), nw.reshape(1, D), msk, s0)


def _bucket_of_distance(dist):
    n = np.maximum(np.asarray(dist), 0)
    max_exact = NBUCKETS // 2
    nf = np.maximum(n, 1).astype(np.float64)
    large = max_exact + (np.log(nf / max_exact) / math.log(MAXDIST / max_exact)
                         * (NBUCKETS - max_exact)).astype(np.int64)
    return np.where(n < max_exact, n, np.minimum(large, NBUCKETS - 1))


def _bias_by_distance(dist, table):
    dist = np.asarray(dist)
    onehot = np.eye(NBUCKETS, dtype=np.float32)[_bucket_of_distance(dist).reshape(-1)]
    out = jnp.einsum("nb,bh->hn", jnp.asarray(onehot), table.astype(F32),
                     precision=lax.Precision.HIGHEST)
    return out.reshape((H,) + dist.shape)


def _toeplitz(fvals, T):
    w = jnp.pad(fvals[:, ::-1], ((0, 0), (0, 1)))
    flat = jnp.tile(w, (1, T))[:, :T * (2 * T - 1)]
    return flat.reshape(H, T, 2 * T - 1)[:, :, T - 1:]


def _subln(d, w):
    return _rms(d, w) * (1.0 - LAMBDA_INIT)


def _attn_prompt_kernel(lam_ref, q_ref, k_ref, v_ref, strip_ref, sw_ref, o_ref, s_scr, m_scr, kb_scr, vx_scr):
    T = AT_T
    starts = list(range(0, LP, T))
    sizes = [min(T, LP - r0) for r0 in starts]
    nq = len(starts)
    nt = (((1,), (1,)), ((), ()))
    kb_scr[...] = k_ref[...].astype(BF16)
    vx_scr[:, :HD] = v_ref[...].astype(BF16)
    vx_scr[:, HD:] = jnp.ones((LP, HD), BF16)

    def step(t):
        slot = t % 2
        if t < nq:
            tq = sizes[t]
            qb = (q_ref[starts[t]:starts[t] + tq, :] * (DQK ** -0.5)).astype(BF16)
            lane = lax.broadcasted_iota(jnp.int32, (tq, HD), 1)
            zero = jnp.zeros_like(qb)
            qq = jnp.concatenate([jnp.where(lane < DQK, qb, zero), jnp.where(lane >= DQK, qb, zero)], axis=0)
            mx = jnp.full((2 * tq, HD), NEG, F32)
        if t > 0:
            tp = sizes[t - 1]
            m_prev = m_scr[1 - slot, :2 * tp, :]
            acc = jnp.zeros((2 * tp, 2 * HD), F32)
        for c in range(min(t + 1, nq)):
            tk = sizes[c]
            if t < nq:
                bt = strip_ref[0, min(t - c, 2), :tq, :tk]
                sc = lax.dot_general(qq, kb_scr[starts[c]:starts[c] + tk, :], nt,
                                     preferred_element_type=F32) + jnp.concatenate([bt, bt], axis=0)
                s_scr[slot, c, :2 * tq, :tk] = sc
                for u in range(tk // HD):
                    mx = jnp.maximum(mx, sc[:, u * HD:(u + 1) * HD])
            if t > 0 and c < t:
                sp = s_scr[1 - slot, c, :2 * tp, :tk]
                p = jnp.exp(jnp.concatenate([sp[:, u * HD:(u + 1) * HD] - m_prev for u in range(tk // HD)],
                                            axis=1))
                acc = acc + jnp.dot(p.astype(BF16), vx_scr[starts[c]:starts[c] + tk, :],
                                    preferred_element_type=F32)
        if t < nq:
            m_scr[slot, :2 * tq, :] = jnp.broadcast_to(jnp.max(mx, axis=-1, keepdims=True), (2 * tq, HD))
        if t > 0:
            o = acc[:, :HD] / acc[:, HD:]
            o_ref[starts[t - 1]:starts[t - 1] + tp, :] = _subln(o[:tp] - lam_ref[0] * o[tp:], sw_ref[...])

    for t in range(nq + 1):
        step(t)


def attn_prompt(qkv, strips, lam, sw):
    T = AT_T
    nq = -(-LP // T)
    smem = pl.BlockSpec(memory_space=pltpu.SMEM)
    return pl.pallas_call(
        _attn_prompt_kernel,
        grid=(B, H),
        in_specs=[smem,
                  pl.BlockSpec((LP, HD), lambda b, h: (b, h)),
                  pl.BlockSpec((LP, HD), lambda b, h: (b, H + h)),
                  pl.BlockSpec((LP, HD), lambda b, h: (b, 2 * H + h)),
                  pl.BlockSpec((1, 3, T, T), lambda b, h: (h, 0, 0, 0)),
                  pl.BlockSpec((1, HD), lambda b, h: (0, 0))],
        out_specs=pl.BlockSpec((LP, HD), lambda b, h: (b, h)),
        out_shape=jax.ShapeDtypeStruct((B * LP, D), F32),
        scratch_shapes=[pltpu.VMEM((2, nq, 2 * T, T), F32), pltpu.VMEM((2, 2 * T, HD), F32),
                        pltpu.VMEM((LP, HD), BF16), pltpu.VMEM((LP, 2 * HD), BF16)],
        compiler_params=_cparams(("parallel", "parallel")),
        name="attn_prompt",
    )(lam, qkv, qkv, qkv, strips, sw.reshape(1, HD))


def _attn_decode_kernel(pt_ref, lam_ref, q_ref, *refs):
    del pt_ref
    k_refs = refs[:PP]
    v_refs = refs[PP:2 * PP]
    bias_ref, kn_ref, vn_ref, bn_ref, sw_ref, o_ref, m_ref, acc_ref, q_scr = refs[2 * PP:]
    g = pl.program_id(1)
    last = pl.num_programs(1) - 1
    nt = (((1,), (1,)), ((), ()))

    @pl.when(g == 0)
    def _():
        m_ref[...] = jnp.full(m_ref.shape, NEG, F32)
        acc_ref[...] = jnp.zeros(acc_ref.shape, F32)
        q8 = q_ref[0]
        r = lax.broadcasted_iota(jnp.int32, (HD, D), 0)
        col = lax.broadcasted_iota(jnp.int32, (HD, D), 1)
        q_scr[...] = jnp.where(r // 8 == col // DQK, jnp.concatenate([q8] * 16, axis=0), 0.0).astype(BF16)

    qrows = q_scr[...]
    kt = jnp.concatenate([k_refs[j][...].astype(BF16) for j in range(PP)], axis=1)
    s = jnp.dot(qrows, kt, preferred_element_type=F32) + bias_ref[0]
    ones = jnp.ones((PP * PAGE, HD), BF16)

    def finish(s_new):
        smax = s[:, :PAGE]
        for j in range(1, PP):
            smax = jnp.maximum(smax, s[:, j * PAGE:(j + 1) * PAGE])
        m_cur = jnp.max(smax, axis=-1, keepdims=True)
        if s_new is not None:
            m_cur = jnp.maximum(m_cur, jnp.max(s_new, axis=-1, keepdims=True))
        m_old = m_ref[...]
        m_new = jnp.maximum(m_old, jnp.broadcast_to(m_cur, (HD, HD)))
        m_ref[...] = m_new
        alpha = jnp.exp(m_old - m_new)
        alpha2 = jnp.concatenate([alpha, alpha], axis=1)
        p = jnp.exp(s - jnp.concatenate([m_new] * PP, axis=1)).astype(BF16)
        if s_new is not None:
            p_new = jnp.exp(s_new - m_new[:, :16]).astype(BF16)
        for hh in range(H):
            rows = slice(hh * 16, (hh + 1) * 16)
            vh = jnp.concatenate([v_refs[j][pl.ds(hh, PAGE, stride=H), :].astype(BF16)
                                  for j in range(PP)], axis=0)
            pv = jnp.dot(p[rows], jnp.concatenate([vh, ones], axis=1), preferred_element_type=F32)
            if s_new is not None:
                vn = vn_ref[0, :, hh * HD:(hh + 1) * HD].astype(BF16)
                pv = pv + jnp.dot(p_new[rows], jnp.concatenate([vn, ones[:16]], axis=1),
                                  preferred_element_type=F32)
            acc_ref[rows, :] = alpha2[rows] * acc_ref[rows, :] + pv

    @pl.when(g < last)
    def _():
        finish(None)

    @pl.when(g == last)
    def _():
        finish(lax.dot_general(qrows, kn_ref[0].astype(BF16), nt,
                               preferred_element_type=F32) + bn_ref[...])
        acc = acc_ref[...]
        o = acc[:, :HD] / acc[:, HD:]
        outs = []
        for hh in range(H):
            o1 = o[hh * 16:hh * 16 + 8]
            o2 = o[hh * 16 + 8:hh * 16 + 16]
            outs.append(_subln(o1 - lam_ref[0] * o2, sw_ref[...]))
        o_ref[0] = jnp.concatenate(outs, axis=1)


def attn_decode(page_table, lam, q8, cache_kt, cache_v2, bias_pages, k_new, v_new, bias_new, sw):
    n_steps = NPAGES // PP

    def page_spec(j):
        return pl.BlockSpec((D, PAGE), lambda b, g, pt: (pt[b * NPAGES + g * PP + j], 0))

    smem = pl.BlockSpec(memory_space=pltpu.SMEM)
    grid_spec = pltpu.PrefetchScalarGridSpec(
        num_scalar_prefetch=1,
        grid=(DB, n_steps),
        in_specs=([smem, pl.BlockSpec((1, 8, D), lambda b, g, pt: (b, 0, 0))]
                  + [page_spec(j) for j in range(PP)]
                  + [page_spec(j) for j in range(PP)]
                  + [pl.BlockSpec((1, HD, PP * PAGE), lambda b, g, pt: (g, 0, 0)),
                     pl.BlockSpec((1, 16, D), lambda b, g, pt: (b, 0, 0)),
                     pl.BlockSpec((1, 16, D), lambda b, g, pt: (b, 0, 0)),
                     pl.BlockSpec((HD, 16), lambda b, g, pt: (0, 0)),
                     pl.BlockSpec((1, HD), lambda b, g, pt: (0, 0))]),
        out_specs=pl.BlockSpec((1, 8, D), lambda b, g, pt: (b, 0, 0)),
        scratch_shapes=[pltpu.VMEM((HD, HD), F32), pltpu.VMEM((HD, 2 * HD), F32),
                        pltpu.VMEM((HD, D), BF16)],
    )
    return pl.pallas_call(
        _attn_decode_kernel,
        grid_spec=grid_spec,
        out_shape=jax.ShapeDtypeStruct((DB, 8, D), F32),
        compiler_params=_cparams(("parallel", "arbitrary")),
        name="attn_decode",
    )(page_table.reshape(-1), lam, q8, *([cache_kt] * PP), *([cache_v2] * PP),
      bias_pages, k_new, v_new, bias_new, sw.reshape(1, HD))


def _router_kernel(x_ref, wn_ref, wr_ref, br_ref, h_ref, idx_ref, gate_ref):
    h = _rms(x_ref[...], wn_ref[...])
    h_ref[...] = h
    h_hi = h.astype(BF16)
    h_lo = (h - h_hi.astype(F32)).astype(BF16)
    logits = (jnp.dot(h_hi, wr_ref[0], preferred_element_type=F32)
              + jnp.dot(h_lo, wr_ref[0], preferred_element_type=F32)
              + jnp.dot(h_hi, wr_ref[1], preferred_element_type=F32)) + br_ref[...]
    lane = lax.broadcasted_iota(jnp.int32, logits.shape, 1)
    ninf = jnp.float32(-jnp.inf)
    logits = jnp.where(lane < NEXP, logits, ninf)
    v1 = jnp.max(logits, axis=-1, keepdims=True)
    i1 = jnp.min(jnp.where(logits == v1, lane, HD), axis=-1, keepdims=True)
    rest = jnp.where(lane == i1, ninf, logits)
    v2 = jnp.max(rest, axis=-1, keepdims=True)
    i2 = jnp.min(jnp.where(rest == v2, lane, HD), axis=-1, keepdims=True)
    e = jnp.exp(v2 - v1)
    g1 = 1.0 / (1.0 + e)
    g2 = e / (1.0 + e)
    idx_ref[...] = jnp.where(lane == 0, i1, jnp.where(lane == 1, i2, 0))
    gate_ref[...] = jnp.where(lane == 0, g1, jnp.where(lane == 1, g2, 0.0))


def moe_router(x, wn, w_router, b_router):
    rows = x.shape[0]
    wr = jnp.zeros((D, HD), F32).at[:, :NEXP].set(w_router.astype(F32))
    wr_hi = wr.astype(BF16)
    wr_lo = (wr - wr_hi.astype(F32)).astype(BF16)
    br = jnp.zeros((1, HD), F32).at[0, :NEXP].set(b_router.astype(F32))
    return pl.pallas_call(
        _router_kernel,
        grid=(rows // TM,),
        in_specs=[pl.BlockSpec((TM, D), lambda i: (i, 0)),
                  pl.BlockSpec((1, D), lambda i: (0, 0)),
                  pl.BlockSpec((2, D, HD), lambda i: (0, 0, 0)),
                  pl.BlockSpec((1, HD), lambda i: (0, 0))],
        out_specs=[pl.BlockSpec((TM, D), lambda i: (i, 0)),
                   pl.BlockSpec((TM, HD), lambda i: (i, 0)),
                   pl.BlockSpec((TM, HD), lambda i: (i, 0))],
        out_shape=[jax.ShapeDtypeStruct((rows, D), F32),
                   jax.ShapeDtypeStruct((rows, HD), jnp.int32),
                   jax.ShapeDtypeStruct((rows, HD), F32)],
        compiler_params=_cparams(("parallel",)),
        name="moe_router",
    )(x, wn.reshape(1, D), jnp.stack([wr_hi, wr_lo]), br)


def _moe_kernel(te_ref, nv_ref, sa_ref, sr_ref, h_hbm, wa_ref, wb_ref, wd_ref, out_hbm,
                hbuf, hb16, obuf, gsem, ssem):
    del te_ref
    i = pl.program_id(0)
    j = pl.program_id(1)
    nt = pl.num_programs(0)
    last_j = pl.num_programs(1) - 1
    slot = lax.rem(i, 2)
    live = nv_ref[i] > 0

    def start_gather(tile, s):
        def body(g, carry):
            base = pl.multiple_of(g * 8, 8)
            for u in range(8):
                row = sr_ref[tile * MOE_TM + base + u]
                pltpu.make_async_copy(h_hbm.at[pl.ds(row, 1)], hbuf.at[s, pl.ds(base + u, 1)],
                                      gsem.at[s]).start()
            return carry
        lax.fori_loop(0, MOE_TM // 8, body, 0)

    def wait_gather(s):
        pltpu.make_async_copy(h_hbm.at[pl.ds(0, MOE_TM)], hbuf.at[s], gsem.at[s]).wait()

    def start_scatter(tile, s):
        def body(g, carry):
            base = pl.multiple_of(g * 8, 8)
            for u in range(8):
                a = sa_ref[tile * MOE_TM + base + u]
                pltpu.make_async_copy(obuf.at[s, pl.ds(base + u, 1)], out_hbm.at[pl.ds(a, 1)],
                                      ssem.at[s]).start()
            return carry
        lax.fori_loop(0, MOE_TM // 8, body, 0)

    def wait_scatter(s):
        pltpu.make_async_copy(obuf.at[s], out_hbm.at[pl.ds(0, MOE_TM)], ssem.at[s]).wait()

    @pl.when(j == 0)
    def _():
        @pl.when((i >= 2) & (nv_ref[jnp.maximum(i - 2, 0)] > 0))
        def _():
            wait_scatter(slot)

        @pl.when(live)
        def _():
            @pl.when(i == 0)
            def _():
                start_gather(0, 0)
                obuf[1] = jnp.zeros((MOE_TM, D), F32)
                for blk in range(2):
                    pltpu.make_async_copy(obuf.at[1], out_hbm.at[pl.ds(2 * R + blk * MOE_TM, MOE_TM)],
                                          ssem.at[1]).start()
                for blk in range(2):
                    wait_scatter(1)

            nxt = jnp.minimum(i + 1, nt - 1)

            @pl.when((i + 1 < nt) & (nv_ref[nxt] > 0))
            def _():
                start_gather(nxt, 1 - slot)

            wait_gather(slot)
            hb16[...] = hbuf[slot].astype(BF16)

    @pl.when(live)
    def _():
        h = hb16[...]
        a = jnp.dot(h, wa_ref[0], preferred_element_type=F32)
        b = jnp.dot(h, wb_ref[0], preferred_element_type=F32)
        act = (a * jax.nn.sigmoid(a) * b).astype(BF16)
        y = jnp.dot(act, wd_ref[0], preferred_element_type=F32)

        @pl.when(j == 0)
        def _():
            obuf[slot] = y

        @pl.when(j > 0)
        def _():
            obuf[slot] += y

        @pl.when(j == last_j)
        def _():
            start_scatter(i, slot)

    @pl.when((i == nt - 1) & (j == last_j))
    def _():
        @pl.when(nv_ref[nt - 2] > 0)
        def _():
            wait_scatter(1 - slot)

        @pl.when(live)
        def _():
            wait_scatter(slot)


def moe_experts(tile_expert, tile_rows, slot_assign, slot_row, h, w_up, w_down):
    nf = DFFE // MOE_FC
    grid_spec = pltpu.PrefetchScalarGridSpec(
        num_scalar_prefetch=4,
        grid=(MOE_NT, nf),
        in_specs=[pl.BlockSpec(memory_space=pl.ANY),
                  pl.BlockSpec((1, D, MOE_FC), lambda i, j, te, nv, sa, sr: (te[i], 0, j)),
                  pl.BlockSpec((1, D, MOE_FC), lambda i, j, te, nv, sa, sr: (te[i], 0, nf + j)),
                  pl.BlockSpec((1, MOE_FC, D), lambda i, j, te, nv, sa, sr: (te[i], j, 0))],
        out_specs=pl.BlockSpec(memory_space=pl.ANY),
        scratch_shapes=[pltpu.VMEM((2, MOE_TM, D), F32), pltpu.VMEM((MOE_TM, D), BF16),
                        pltpu.VMEM((2, MOE_TM, D), F32),
                        pltpu.SemaphoreType.DMA((2,)), pltpu.SemaphoreType.DMA((2,))],
    )
    return pl.pallas_call(
        _moe_kernel,
        grid_spec=grid_spec,
        out_shape=jax.ShapeDtypeStruct((2 * R + 2 * MOE_TM, D), F32),
        compiler_params=_cparams(("arbitrary", "arbitrary")),
        name="moe_experts",
    )(tile_expert, tile_rows, slot_assign, slot_row, h, w_up, w_up, w_down)


def _combine_kernel(x_ref, g_ref, e0_ref, e1_ref, wn_ref, y_ref):
    moe = g_ref[:, 0:1] * e0_ref[...] + g_ref[:, 1:2] * e1_ref[...]
    y_ref[...] = _rms(x_ref[...] + moe, wn_ref[...])


def moe_combine_norm(x, gates, eo, wn):
    rows = x.shape[0]
    nb = rows // TM
    return pl.pallas_call(
        _combine_kernel,
        grid=(rows // TM,),
        in_specs=[pl.BlockSpec((TM, D), lambda i: (i, 0)),
                  pl.BlockSpec((TM, HD), lambda i: (i, 0)),
                  pl.BlockSpec((TM, D), lambda i: (i, 0)),
                  pl.BlockSpec((TM, D), lambda i: (nb + i, 0)),
                  pl.BlockSpec((1, D), lambda i: (0, 0))],
        out_specs=pl.BlockSpec((TM, D), lambda i: (i, 0)),
        out_shape=jax.ShapeDtypeStruct((rows, D), F32),
        compiler_params=_cparams(("parallel",)),
        name="moe_combine_norm",
    )(x, gates, eo, eo, wn.reshape(1, D))


def _moe_plan(idx):
    e_flat = idx.reshape(-1)
    onehot = (e_flat[:, None] == jnp.arange(NEXP, dtype=jnp.int32)[None, :]).astype(jnp.int32)
    cum = jnp.cumsum(onehot, axis=0)
    counts = cum[-1]
    padded = ((counts + MOE_TM - 1) // MOE_TM) * MOE_TM
    ends = jnp.cumsum(padded)
    starts = ends - padded
    slot = jnp.sum(onehot * (starts[None, :] + cum - onehot), axis=1)
    pos = jnp.arange(MOE_NT * MOE_TM, dtype=jnp.int32)
    spare = 2 * R + ((pos // MOE_TM) % 2) * MOE_TM + pos % MOE_TM
    flat = jnp.arange(2 * R, dtype=jnp.int32)
    slot_assign = spare.at[slot].set((flat % 2) * R + flat // 2)
    tile_start = jnp.arange(MOE_NT, dtype=jnp.int32) * MOE_TM
    owner = (tile_start[:, None] >= starts[None, :]) & (tile_start[:, None] < ends[None, :])
    tile_rows = jnp.sum(jnp.where(owner, jnp.clip(counts[None, :] - (tile_start[:, None] - starts[None, :]),
                                                  0, MOE_TM), 0), axis=1).astype(jnp.int32)
    last_expert = jnp.max(jnp.where(counts > 0, jnp.arange(NEXP, dtype=jnp.int32), 0))
    tile_expert = jnp.where(jnp.any(owner, axis=1), jnp.argmax(owner, axis=1), last_expert).astype(jnp.int32)
    slot_row = jnp.where(slot_assign < 2 * R, slot_assign % R, 0).astype(jnp.int32)
    return tile_expert, tile_rows, slot_assign, slot_row


def kernel(x_prompt, x_sample, state_hgrn, cache_k, cache_v, page_table, meta_tokens,
           norm_mix_w, norm_ffn_w, hg_w_in, hg_lower_bound, hg_norm_w, hg_w_out,
           at_w_qkv, at_lambda_q1, at_lambda_k1, at_lambda_q2, at_lambda_k2, at_subln_w,
           at_w_out, rel_bias_table, ff_w_up, ff_w_down, moe_w_router, moe_b_router,
           moe_w_up, moe_w_down, final_norm_w):
    xp = jnp.concatenate([jnp.broadcast_to(meta_tokens.astype(F32), (B, NMETA, D)), x_prompt,
                          jnp.zeros((B, LP - LREAL, D), F32)], axis=1)
    x0 = jnp.concatenate([xp.reshape(RP, D), x_sample.reshape(RS, D)], axis=0)
    lower = jnp.cumsum(jax.nn.softmax(hg_lower_bound.astype(F32), axis=0), axis=0)

    w_in = hg_w_in[0].astype(BF16)
    y_p, st_p = hgrn_mixer_fused(x0, norm_mix_w[0], w_in, lower[0], hg_norm_w[0],
                                 B, HG_C, LP // HG_C, LREAL)
    proj_s = norm_matmul(x0[RP:], norm_mix_w[0], w_in, 1024)
    proj_s = jnp.pad(proj_s.reshape(DB, DT, 4 * D), ((0, 0), (0, HG_CS - DT), (0, 0)))
    y_s, st_s = hgrn_mixer(proj_s.reshape(DB * HG_CS, 4 * D), lower[0], hg_norm_w[0], state_hgrn[0],
                           DB, HG_CS, 1, DT, True, n_seq=HG_NS)
    x1 = matmul_residual(y_p, y_s.reshape(DB, HG_CS, D)[:, :DT].reshape(RS, D),
                         hg_w_out[0].astype(BF16), x0)
    x2 = dense_ffn(x1, norm_ffn_w[0], ff_w_up[0].astype(BF16), ff_w_down[0].astype(BF16))

    f = F32
    lam = (jnp.exp(jnp.sum(at_lambda_q1[0].astype(f) * at_lambda_k1[0].astype(f)))
           - jnp.exp(jnp.sum(at_lambda_q2[0].astype(f) * at_lambda_k2[0].astype(f))) + LAMBDA_INIT)
    lam = lam.reshape(1).astype(F32)
    qkv = norm_matmul(x2, norm_mix_w[1], at_w_qkv[0].astype(BF16), 1024)
    q_all, k_all, v_all = qkv[:, :D], qkv[:, D:2 * D], qkv[:, 2 * D:]
    qscale = DQK ** -0.5

    T = AT_T
    dvals = np.arange(-(T - 1), T)
    f_diag = jnp.where(jnp.asarray(dvals >= 0)[None, :], _bias_by_distance(dvals, rel_bias_table), NEG)
    f_prev = _bias_by_distance(dvals + T, rel_bias_table)
    cfar = rel_bias_table[NBUCKETS - 1].astype(F32)
    strips = jnp.stack([_toeplitz(f_diag, T), _toeplitz(f_prev, T),
                        jnp.broadcast_to(cfar[:, None, None], (H, T, T))], axis=1)
    o_p = attn_prompt(qkv, strips, lam, at_subln_w[0])

    q8 = jnp.pad((q_all[RP:] * qscale).reshape(DB, DT, D), ((0, 0), (0, 8 - DT), (0, 0)))
    tq = np.minimum(np.arange(8), DT - 1)
    kpos = np.arange(PAST).reshape(NPAGES, 1, PAGE)
    bp = _bias_by_distance(PAST + tq[None, :, None] - kpos, rel_bias_table)
    bias_pages = jnp.broadcast_to(bp.transpose(1, 0, 2, 3)[:, :, None], (NPAGES, H, 2, 8, PAGE))
    bias_pages = bias_pages.reshape(NPAGES // PP, PP, HD, PAGE).transpose(0, 2, 1, 3)
    bias_pages = bias_pages.reshape(NPAGES // PP, HD, PP * PAGE)
    sn = np.arange(16)
    dn = tq[:, None] - sn[None, :]
    bn = jnp.where(jnp.asarray((dn >= 0) & (sn[None, :] < DT))[None],
                   _bias_by_distance(dn, rel_bias_table), NEG)
    bias_new = jnp.broadcast_to(bn[:, None], (H, 2, 8, 16)).reshape(HD, 16)
    k_new = jnp.pad(k_all[RP:].reshape(DB, DT, D), ((0, 0), (0, 16 - DT), (0, 0)))
    v_new = jnp.pad(v_all[RP:].reshape(DB, DT, D), ((0, 0), (0, 16 - DT), (0, 0)))
    n_phys = cache_k.shape[1]
    cache_kt = jnp.transpose(cache_k[0], (0, 2, 3, 4, 1)).reshape(n_phys * D, PAGE)
    cache_v2 = cache_v[0].reshape(n_phys * PAGE * H, HD)
    o_s = attn_decode(page_table, lam, q8, cache_kt, cache_v2, bias_pages, k_new, v_new, bias_new,
                      at_subln_w[0])
    x3 = matmul_residual(o_p, o_s[:, :DT].reshape(RS, D), at_w_out[0].astype(BF16), x2)

    h_moe, idx, gates = moe_router(x3, norm_ffn_w[1], moe_w_router[0], moe_b_router[0])
    tile_expert, tile_rows, slot_assign, slot_row = _moe_plan(idx[:, :TOPK])
    eo = moe_experts(tile_expert, tile_rows, slot_assign, slot_row, h_moe,
                     moe_w_up[0].astype(BF16), moe_w_down[0].astype(BF16))
    y = moe_combine_norm(x3, gates, eo, final_norm_w)

    y_prompt = y[:RP].reshape(B, LP, D)[:, NMETA:LREAL]
    y_sample = y[RP:].reshape(DB, DT, D)
    k_p = k_all[:RP].reshape(B, LP, H, 2, DQK)[:, :LREAL]
    v_p = v_all[:RP].reshape(B, LP, H, HD)[:, :LREAL]
    k_s = k_all[RP:].reshape(DB, DT, H, 2, DQK)
    v_s = v_all[RP:].reshape(DB, DT, H, HD)
    return (y_prompt, y_sample, st_p[None], st_s[None], k_p[None], v_p[None], k_s[None], v_s[None])
```

```python
import functools
import math

import numpy as np
import jax
import jax.numpy as jnp
from jax import lax
from jax.experimental import pallas as pl
from jax.experimental.pallas import tpu as pltpu

F32 = jnp.float32
BF16 = jnp.bfloat16

D = 1024
B = 8
SEQ = 2048
DB = 128
DT = 4
PAST = 2048
PAGE = 128
NPAGES = PAST // PAGE
NMETA = 16
LREAL = NMETA + SEQ
LP = 2176
RP = B * LP
RS = DB * DT
R = RP + RS
H = 8
HD = 128
DQK = 64
NBUCKETS = 32
MAXDIST = 128
DFF = 2816
NEXP = 8
TOPK = 2
DFFE = 3584
EPS = 1e-6
NEG = -1e30
LAMBDA_INIT = 0.8 - 0.6 * math.exp(-0.3 * 1)

VMEM_LIMIT = 56 * 1024 * 1024

TM = 512
HG_C = 128
HG_CS = 16
HG_NS = 4
AT_T = 256
MOE_TM = 512
MOE_NT = (2 * R) // MOE_TM + NEXP
MOE_FC = 1792
FFN_FC = 1408
PP = 8


def _cparams(sem, vmem=VMEM_LIMIT):
    return pltpu.CompilerParams(dimension_semantics=sem, vmem_limit_bytes=vmem)


def _rms(x, w):
    return (x * lax.rsqrt(jnp.mean(x * x, axis=-1, keepdims=True) + EPS)) * w


def _norm_matmul_kernel(x_ref, wn_ref, w_ref, o_ref, h_ref):
    @pl.when(pl.program_id(1) == 0)
    def _():
        h_ref[...] = _rms(x_ref[...], wn_ref[...]).astype(BF16)

    o_ref[...] = jnp.dot(h_ref[...], w_ref[...], preferred_element_type=F32)


def norm_matmul(x, wn, w, tn):
    rows, n = x.shape[0], w.shape[1]
    return pl.pallas_call(
        _norm_matmul_kernel,
        grid=(rows // TM, n // tn),
        in_specs=[pl.BlockSpec((TM, D), lambda i, j: (i, 0)),
                  pl.BlockSpec((1, D), lambda i, j: (0, 0)),
                  pl.BlockSpec((D, tn), lambda i, j: (0, j))],
        out_specs=pl.BlockSpec((TM, tn), lambda i, j: (i, j)),
        out_shape=jax.ShapeDtypeStruct((rows, n), F32),
        scratch_shapes=[pltpu.VMEM((TM, D), BF16)],
        compiler_params=_cparams(("parallel", "arbitrary")),
        name="norm_matmul",
    )(x, wn.reshape(1, D), w)


def _matmul_residual_kernel(ap_ref, as_ref, w_ref, x_ref, o_ref, *, n_prompt_tiles):
    def emit(a_ref):
        o_ref[...] = x_ref[...] + jnp.dot(a_ref[...].astype(BF16), w_ref[...],
                                          preferred_element_type=F32)

    i = pl.program_id(0)
    pl.when(i < n_prompt_tiles)(lambda: emit(ap_ref))
    pl.when(i >= n_prompt_tiles)(lambda: emit(as_ref))


def matmul_residual(a_prompt, a_decode, w, x):
    rows = x.shape[0]
    n_p = a_prompt.shape[0] // TM
    return pl.pallas_call(
        functools.partial(_matmul_residual_kernel, n_prompt_tiles=n_p),
        grid=(rows // TM,),
        in_specs=[pl.BlockSpec((TM, D), lambda i: (jnp.minimum(i, n_p - 1), 0)),
                  pl.BlockSpec((TM, D), lambda i: (jnp.maximum(i - n_p, 0), 0)),
                  pl.BlockSpec((D, D), lambda i: (0, 0)),
                  pl.BlockSpec((TM, D), lambda i: (i, 0))],
        out_specs=pl.BlockSpec((TM, D), lambda i: (i, 0)),
        out_shape=jax.ShapeDtypeStruct((rows, D), F32),
        compiler_params=_cparams(("parallel",)),
        name="matmul_residual",
    )(a_prompt, a_decode, w, x)


def _ffn_kernel(x_ref, wn_ref, wa_ref, wb_ref, wd_ref, o_ref, h_ref):
    j = pl.program_id(1)

    @pl.when(j == 0)
    def _():
        h_ref[...] = _rms(x_ref[...], wn_ref[...]).astype(BF16)

    h = h_ref[...]
    a = jnp.dot(h, wa_ref[...], preferred_element_type=F32)
    b = jnp.dot(h, wb_ref[...], preferred_element_type=F32)
    act = (a * jax.nn.sigmoid(a) * b).astype(BF16)
    y = jnp.dot(act, wd_ref[...], preferred_element_type=F32)

    @pl.when(j == 0)
    def _():
        o_ref[...] = x_ref[...] + y

    @pl.when(j > 0)
    def _():
        o_ref[...] += y


def dense_ffn(x, wn, w_up, w_down):
    rows = x.shape[0]
    nf = DFF // FFN_FC
    return pl.pallas_call(
        _ffn_kernel,
        grid=(rows // TM, nf),
        in_specs=[pl.BlockSpec((TM, D), lambda i, j: (i, 0)),
                  pl.BlockSpec((1, D), lambda i, j: (0, 0)),
                  pl.BlockSpec((D, FFN_FC), lambda i, j: (0, j)),
                  pl.BlockSpec((D, FFN_FC), lambda i, j: (0, nf + j)),
                  pl.BlockSpec((FFN_FC, D), lambda i, j: (j, 0))],
        out_specs=pl.BlockSpec((TM, D), lambda i, j: (i, 0)),
        out_shape=jax.ShapeDtypeStruct((rows, D), F32),
        scratch_shapes=[pltpu.VMEM((TM, D), BF16)],
        compiler_params=_cparams(("parallel", "arbitrary")),
        name="dense_ffn",
    )(x, wn.reshape(1, D), w_up, w_up, w_down)


def _hgrn_masks(C):
    nlev = int(math.log2(C))
    t = np.arange(C)[:, None]
    s = np.arange(C)[None, :]
    m = [(t == s)]
    for lev in range(nlev):
        b = 1 << lev
        m.append((t // (2 * b) == s // (2 * b)) & ((t // b) % 2 == 1) & ((s // b) % 2 == 0))
    return np.stack(m).astype(np.float32)


def _hgrn_kernel(proj_ref, lb_ref, nw_ref, msk_ref, s0_ref, y_ref, s_ref, *, C, n_valid, n_seq):
    for bb in range(n_seq):
        _hgrn_chunk(proj_ref.at[pl.ds(bb * C, C)], lb_ref, nw_ref, msk_ref, s0_ref.at[pl.ds(bb, 1)],
                    y_ref.at[pl.ds(bb * C, C)], s_ref.at[pl.ds(bb, 1)], C, n_valid)


def _hgrn_chunk(proj_ref, lb_ref, nw_ref, msk_ref, s0_ref, y_ref, s_ref, C, n_valid):
    nlev = int(math.log2(C))
    c = pl.program_id(1)

    @pl.when(c == 0)
    def _():
        s_ref[...] = s0_ref[...]

    rloc = lax.broadcasted_iota(jnp.int32, (C, HD), 0)
    valid = (c * C + rloc) < n_valid
    nt = (((1,), (1,)), ((), ()))
    tn = (((0,), (0,)), ((), ()))
    o_heads = []
    for h in range(H):
        col = slice(h * HD, (h + 1) * HD)
        lb = lb_ref[:, col]
        q = proj_ref[:, h * HD:(h + 1) * HD]
        f = proj_ref[:, D + h * HD:D + (h + 1) * HD]
        v = proj_ref[:, 2 * D + h * HD:2 * D + (h + 1) * HD]
        qs = q * jax.nn.sigmoid(q)
        fg = lb + (1.0 - lb) * jax.nn.sigmoid(f)
        k = jnp.where(valid, 1.0 - fg, 0.0)
        g = jnp.where(valid, jnp.log2(fg), 0.0)
        G = g
        sh = 1
        while sh < C:
            G = G + jnp.where(rloc >= sh, pltpu.roll(G, sh, 0), 0.0)
            sh *= 2
        vb = v.astype(BF16)
        A = msk_ref[0] * lax.dot_general(qs.astype(BF16), k.astype(BF16), nt,
                                         preferred_element_type=F32)
        for lev in range(nlev):
            b = 1 << lev
            if b >= 8:
                gm = G.reshape(C // (2 * b), 2 * b, HD)[:, b - 1:b, :]
                gmid = jnp.broadcast_to(gm, (C // (2 * b), 2 * b, HD)).reshape(C, HD)
            else:
                g8 = G.reshape(C // 8, 8, HD)
                sub = lax.broadcasted_iota(jnp.int32, (C // 8, 8, HD), 1)
                gmid = None
                for base in range(0, 8, 2 * b):
                    row = jnp.broadcast_to(g8[:, base + b - 1:base + b, :], (C // 8, 8, HD))
                    gmid = row if gmid is None else jnp.where(sub >= base, row, gmid)
                gmid = gmid.reshape(C, HD)
            right = (rloc // b) % 2 == 1
            fac = jnp.exp2(jnp.where(right, G - gmid, gmid - G))
            qt = jnp.where(right, qs * fac, 0.0).astype(BF16)
            kt = jnp.where(right, 0.0, k * fac).astype(BF16)
            A = A + msk_ref[lev + 1] * lax.dot_general(qt, kt, nt, preferred_element_type=F32)
        g_end = G[C - 1:C, :]
        qh = (qs * jnp.exp2(G)).astype(BF16)
        kh = (k * jnp.exp2(g_end - G)).astype(BF16)
        S = s_ref[0, h]
        o_h = (jnp.dot(A.astype(BF16), vb, preferred_element_type=F32)
               + jnp.dot(qh, S.astype(BF16), preferred_element_type=F32))
        e_col = jnp.broadcast_to(jnp.exp2(g_end), (HD, HD)).T
        s_ref[0, h] = e_col * S + lax.dot_general(kh, vb, tn, preferred_element_type=F32)
        o_heads.append(o_h)
    o = jnp.concatenate(o_heads, axis=1)
    gate = proj_ref[:, 3 * D:4 * D]
    y_ref[...] = _rms(o * jax.nn.sigmoid(gate), nw_ref[...]).astype(BF16)


def _hgrn_fused_kernel(x_ref, xn_ref, wn_ref, w_ref, lb_ref, nw_ref, msk_ref, s0_ref, y_ref, s_ref,
                       proj_scr, *, C, n_valid):
    c = pl.program_id(1)
    slot = lax.rem(c, 2)

    def project(x):
        return jnp.dot(_rms(x, wn_ref[...]).astype(BF16), w_ref[...], preferred_element_type=F32)

    @pl.when(c == 0)
    def _():
        proj_scr[0] = project(x_ref[...])

    proj_scr[1 - slot] = project(xn_ref[...])
    _hgrn_chunk(proj_scr.at[slot], lb_ref, nw_ref, msk_ref, s0_ref, y_ref, s_ref, C, n_valid)


def hgrn_mixer_fused(x, wn, w_in, lb, nw, nb, C, n_chunks, n_valid):
    msk = jnp.asarray(_hgrn_masks(C))
    nlev1 = msk.shape[0]
    const2 = lambda b, c: (0, 0)
    return pl.pallas_call(
        functools.partial(_hgrn_fused_kernel, C=C, n_valid=n_valid),
        grid=(nb, n_chunks),
        in_specs=[pl.BlockSpec((C, D), lambda b, c: (b * n_chunks + c, 0)),
                  pl.BlockSpec((C, D), lambda b, c: (b * n_chunks + jnp.minimum(c + 1, n_chunks - 1), 0)),
                  pl.BlockSpec((1, D), const2),
                  pl.BlockSpec((D, 4 * D), const2),
                  pl.BlockSpec((1, D), const2),
                  pl.BlockSpec((1, D), const2),
                  pl.BlockSpec((nlev1, C, C), lambda b, c: (0, 0, 0)),
                  pl.BlockSpec((1, H, HD, HD), lambda b, c: (0, 0, 0, 0))],
        out_specs=[pl.BlockSpec((C, D), lambda b, c: (b * n_chunks + c, 0)),
                   pl.BlockSpec((1, H, HD, HD), lambda b, c: (b, 0, 0, 0))],
        out_shape=[jax.ShapeDtypeStruct((nb * n_chunks * C, D), BF16),
                   jax.ShapeDtypeStruct((nb, H, HD, HD), F32)],
        scratch_shapes=[pltpu.VMEM((2, C, 4 * D), F32)],
        compiler_params=_cparams(("parallel", "arbitrary")),
        name="hgrn_mixer_fused",
    )(x, x, wn.reshape(1, D), w_in, lb.reshape(1, D), nw.reshape(1, D), msk,
      jnp.zeros((1, H, HD, HD), F32))


def hgrn_mixer(proj, lb, nw, s0, nb, C, n_chunks, n_valid, s0_per_batch, n_seq=1):
    assert n_seq == 1 or (n_chunks == 1 and s0_per_batch and nb % n_seq == 0)
    msk = jnp.asarray(_hgrn_masks(C))
    nlev1 = msk.shape[0]
    s0_map = (lambda b, c: (b, 0, 0, 0)) if s0_per_batch else (lambda b, c: (0, 0, 0, 0))
    return pl.pallas_call(
        functools.partial(_hgrn_kernel, C=C, n_valid=n_valid, n_seq=n_seq),
        grid=(nb // n_seq, n_chunks),
        in_specs=[pl.BlockSpec((n_seq * C, 4 * D), lambda b, c: (b * n_chunks + c, 0)),
                  pl.BlockSpec((1, D), lambda b, c: (0, 0)),
                  pl.BlockSpec((1, D), lambda b, c: (0, 0)),
                  pl.BlockSpec((nlev1, C, C), lambda b, c: (0, 0, 0)),
                  pl.BlockSpec((n_seq, H, HD, HD), s0_map)],
        out_specs=[pl.BlockSpec((n_seq * C, D), lambda b, c: (b * n_chunks + c, 0)),
                   pl.BlockSpec((n_seq, H, HD, HD), lambda b, c: (b, 0, 0, 0))],
        out_shape=[jax.ShapeDtypeStruct((nb * n_chunks * C, D), BF16),
                   jax.ShapeDtypeStruct((nb, H, HD, HD), F32)],
        compiler_params=_cparams(("parallel", "arbitrary")),
        name="hgrn_mixer",
    )(proj, lb.reshape(1, D), nw.reshape(1, D), msk, s0)


def _bucket_of_distance(dist):
    n = np.maximum(np.asarray(dist), 0)
    max_exact = NBUCKETS // 2
    nf = np.maximum(n, 1).astype(np.float64)
    large = max_exact + (np.log(nf / max_exact) / math.log(MAXDIST / max_exact)
                         * (NBUCKETS - max_exact)).astype(np.int64)
    return np.where(n < max_exact, n, np.minimum(large, NBUCKETS - 1))


def _bias_by_distance(dist, table):
    dist = np.asarray(dist)
    onehot = np.eye(NBUCKETS, dtype=np.float32)[_bucket_of_distance(dist).reshape(-1)]
    out = jnp.einsum("nb,bh->hn", jnp.asarray(onehot), table.astype(F32),
                     precision=lax.Precision.HIGHEST)
    return out.reshape((H,) + dist.shape)


def _toeplitz(fvals, T):
    w = jnp.pad(fvals[:, ::-1], ((0, 0), (0, 1)))
    flat = jnp.tile(w, (1, T))[:, :T * (2 * T - 1)]
    return flat.reshape(H, T, 2 * T - 1)[:, :, T - 1:]


def _subln(d, w):
    return _rms(d, w) * (1.0 - LAMBDA_INIT)


def _attn_prompt_kernel(lam_ref, q_ref, k_ref, v_ref, strip_ref, sw_ref, o_ref, s_scr, m_scr, kb_scr, vx_scr):
    T = AT_T
    starts = list(range(0, LP, T))
    sizes = [min(T, LP - r0) for r0 in starts]
    nq = len(starts)
    nt = (((1,), (1,)), ((), ()))
    kb_scr[...] = k_ref[...].astype(BF16)
    vx_scr[:, :HD] = v_ref[...].astype(BF16)
    vx_scr[:, HD:] = jnp.ones((LP, HD), BF16)

    def step(t):
        slot = t % 2
        if t < nq:
            tq = sizes[t]
            qb = (q_ref[starts[t]:starts[t] + tq, :] * (DQK ** -0.5)).astype(BF16)
            lane = lax.broadcasted_iota(jnp.int32, (tq, HD), 1)
            zero = jnp.zeros_like(qb)
            qq = jnp.concatenate([jnp.where(lane < DQK, qb, zero), jnp.where(lane >= DQK, qb, zero)], axis=0)
            mx = jnp.full((2 * tq, HD), NEG, F32)
        if t > 0:
            tp = sizes[t - 1]
            m_prev = m_scr[1 - slot, :2 * tp, :]
            acc = jnp.zeros((2 * tp, 2 * HD), F32)
        for c in range(min(t + 1, nq)):
            tk = sizes[c]
            if t < nq:
                bt = strip_ref[0, min(t - c, 2), :tq, :tk]
                sc = lax.dot_general(qq, kb_scr[starts[c]:starts[c] + tk, :], nt,
                                     preferred_element_type=F32) + jnp.concatenate([bt, bt], axis=0)
                s_scr[slot, c, :2 * tq, :tk] = sc
                for u in range(tk // HD):
                    mx = jnp.maximum(mx, sc[:, u * HD:(u + 1) * HD])
            if t > 0 and c < t:
                sp = s_scr[1 - slot, c, :2 * tp, :tk]
                p = jnp.exp(jnp.concatenate([sp[:, u * HD:(u + 1) * HD] - m_prev for u in range(tk // HD)],
                                            axis=1))
                acc = acc + jnp.dot(p.astype(BF16), vx_scr[starts[c]:starts[c] + tk, :],
                                    preferred_element_type=F32)
        if t < nq:
            m_scr[slot, :2 * tq, :] = jnp.broadcast_to(jnp.max(mx, axis=-1, keepdims=True), (2 * tq, HD))
        if t > 0:
            o = acc[:, :HD] / acc[:, HD:]
            o_ref[starts[t - 1]:starts[t - 1] + tp, :] = _subln(o[:tp] - lam_ref[0] * o[tp:], sw_ref[...])

    for t in range(nq + 1):
        step(t)


def attn_prompt(qkv, strips, lam, sw):
    T = AT_T
    nq = -(-LP // T)
    smem = pl.BlockSpec(memory_space=pltpu.SMEM)
    return pl.pallas_call(
        _attn_prompt_kernel,
        grid=(B, H),
        in_specs=[smem,
                  pl.BlockSpec((LP, HD), lambda b, h: (b, h)),
                  pl.BlockSpec((LP, HD), lambda b, h: (b, H + h)),
                  pl.BlockSpec((LP, HD), lambda b, h: (b, 2 * H + h)),
                  pl.BlockSpec((1, 3, T, T), lambda b, h: (h, 0, 0, 0)),
                  pl.BlockSpec((1, HD), lambda b, h: (0, 0))],
        out_specs=pl.BlockSpec((LP, HD), lambda b, h: (b, h)),
        out_shape=jax.ShapeDtypeStruct((B * LP, D), F32),
        scratch_shapes=[pltpu.VMEM((2, nq, 2 * T, T), F32), pltpu.VMEM((2, 2 * T, HD), F32),
                        pltpu.VMEM((LP, HD), BF16), pltpu.VMEM((LP, 2 * HD), BF16)],
        compiler_params=_cparams(("parallel", "parallel")),
        name="attn_prompt",
    )(lam, qkv, qkv, qkv, strips, sw.reshape(1, HD))


def _attn_decode_kernel(pt_ref, lam_ref, q_ref, *refs):
    del pt_ref
    k_refs = refs[:PP]
    v_refs = refs[PP:2 * PP]
    bias_ref, kn_ref, vn_ref, bn_ref, sw_ref, o_ref, m_ref, acc_ref, q_scr = refs[2 * PP:]
    g = pl.program_id(1)
    last = pl.num_programs(1) - 1
    nt = (((1,), (1,)), ((), ()))

    @pl.when(g == 0)
    def _():
        m_ref[...] = jnp.full(m_ref.shape, NEG, F32)
        acc_ref[...] = jnp.zeros(acc_ref.shape, F32)
        q8 = q_ref[0]
        r = lax.broadcasted_iota(jnp.int32, (HD, D), 0)
        col = lax.broadcasted_iota(jnp.int32, (HD, D), 1)
        q_scr[...] = jnp.where(r // 8 == col // DQK, jnp.concatenate([q8] * 16, axis=0), 0.0).astype(BF16)

    qrows = q_scr[...]
    kt = jnp.concatenate([k_refs[j][...].astype(BF16) for j in range(PP)], axis=1)
    s = jnp.dot(qrows, kt, preferred_element_type=F32) + bias_ref[0]
    ones = jnp.ones((PP * PAGE, HD), BF16)

    def finish(s_new):
        smax = s[:, :PAGE]
        for j in range(1, PP):
            smax = jnp.maximum(smax, s[:, j * PAGE:(j + 1) * PAGE])
        m_cur = jnp.max(smax, axis=-1, keepdims=True)
        if s_new is not None:
            m_cur = jnp.maximum(m_cur, jnp.max(s_new, axis=-1, keepdims=True))
        m_old = m_ref[...]
        m_new = jnp.maximum(m_old, jnp.broadcast_to(m_cur, (HD, HD)))
        m_ref[...] = m_new
        alpha = jnp.exp(m_old - m_new)
        alpha2 = jnp.concatenate([alpha, alpha], axis=1)
        p = jnp.exp(s - jnp.concatenate([m_new] * PP, axis=1)).astype(BF16)
        if s_new is not None:
            p_new = jnp.exp(s_new - m_new[:, :16]).astype(BF16)
        for hh in range(H):
            rows = slice(hh * 16, (hh + 1) * 16)
            vh = jnp.concatenate([v_refs[j][pl.ds(hh, PAGE, stride=H), :].astype(BF16)
                                  for j in range(PP)], axis=0)
            pv = jnp.dot(p[rows], jnp.concatenate([vh, ones], axis=1), preferred_element_type=F32)
            if s_new is not None:
                vn = vn_ref[0, :, hh * HD:(hh + 1) * HD].astype(BF16)
                pv = pv + jnp.dot(p_new[rows], jnp.concatenate([vn, ones[:16]], axis=1),
                                  preferred_element_type=F32)
            acc_ref[rows, :] = alpha2[rows] * acc_ref[rows, :] + pv

    @pl.when(g < last)
    def _():
        finish(None)

    @pl.when(g == last)
    def _():
        finish(lax.dot_general(qrows, kn_ref[0].astype(BF16), nt,
                               preferred_element_type=F32) + bn_ref[...])
        acc = acc_ref[...]
        o = acc[:, :HD] / acc[:, HD:]
        outs = []
        for hh in range(H):
            o1 = o[hh * 16:hh * 16 + 8]
            o2 = o[hh * 16 + 8:hh * 16 + 16]
            outs.append(_subln(o1 - lam_ref[0] * o2, sw_ref[...]))
        o_ref[0] = jnp.concatenate(outs, axis=1)


def attn_decode(page_table, lam, q8, cache_kt, cache_v2, bias_pages, k_new, v_new, bias_new, sw):
    n_steps = NPAGES // PP

    def page_spec(j):
        return pl.BlockSpec((D, PAGE), lambda b, g, pt: (pt[b * NPAGES + g * PP + j], 0))

    smem = pl.BlockSpec(memory_space=pltpu.SMEM)
    grid_spec = pltpu.PrefetchScalarGridSpec(
        num_scalar_prefetch=1,
        grid=(DB, n_steps),
        in_specs=([smem, pl.BlockSpec((1, 8, D), lambda b, g, pt: (b, 0, 0))]
                  + [page_spec(j) for j in range(PP)]
                  + [page_spec(j) for j in range(PP)]
                  + [pl.BlockSpec((1, HD, PP * PAGE), lambda b, g, pt: (g, 0, 0)),
                     pl.BlockSpec((1, 16, D), lambda b, g, pt: (b, 0, 0)),
                     pl.BlockSpec((1, 16, D), lambda b, g, pt: (b, 0, 0)),
                     pl.BlockSpec((HD, 16), lambda b, g, pt: (0, 0)),
                     pl.BlockSpec((1, HD), lambda b, g, pt: (0, 0))]),
        out_specs=pl.BlockSpec((1, 8, D), lambda b, g, pt: (b, 0, 0)),
        scratch_shapes=[pltpu.VMEM((HD, HD), F32), pltpu.VMEM((HD, 2 * HD), F32),
                        pltpu.VMEM((HD, D), BF16)],
    )
    return pl.pallas_call(
        _attn_decode_kernel,
        grid_spec=grid_spec,
        out_shape=jax.ShapeDtypeStruct((DB, 8, D), F32),
        compiler_params=_cparams(("parallel", "arbitrary")),
        name="attn_decode",
    )(page_table.reshape(-1), lam, q8, *([cache_kt] * PP), *([cache_v2] * PP),
      bias_pages, k_new, v_new, bias_new, sw.reshape(1, HD))


def _router_kernel(x_ref, wn_ref, wr_ref, br_ref, h_ref, idx_ref, gate_ref):
    h = _rms(x_ref[...], wn_ref[...])
    h_ref[...] = h
    h_hi = h.astype(BF16)
    h_lo = (h - h_hi.astype(F32)).astype(BF16)
    logits = (jnp.dot(h_hi, wr_ref[0], preferred_element_type=F32)
              + jnp.dot(h_lo, wr_ref[0], preferred_element_type=F32)
              + jnp.dot(h_hi, wr_ref[1], preferred_element_type=F32)) + br_ref[...]
    lane = lax.broadcasted_iota(jnp.int32, logits.shape, 1)
    ninf = jnp.float32(-jnp.inf)
    logits = jnp.where(lane < NEXP, logits, ninf)
    v1 = jnp.max(logits, axis=-1, keepdims=True)
    i1 = jnp.min(jnp.where(logits == v1, lane, HD), axis=-1, keepdims=True)
    rest = jnp.where(lane == i1, ninf, logits)
    v2 = jnp.max(rest, axis=-1, keepdims=True)
    i2 = jnp.min(jnp.where(rest == v2, lane, HD), axis=-1, keepdims=True)
    e = jnp.exp(v2 - v1)
    g1 = 1.0 / (1.0 + e)
    g2 = e / (1.0 + e)
    idx_ref[...] = jnp.where(lane == 0, i1, jnp.where(lane == 1, i2, 0))
    gate_ref[...] = jnp.where(lane == 0, g1, jnp.where(lane == 1, g2, 0.0))


def moe_router(x, wn, w_router, b_router):
    rows = x.shape[0]
    wr = jnp.zeros((D, HD), F32).at[:, :NEXP].set(w_router.astype(F32))
    wr_hi = wr.astype(BF16)
    wr_lo = (wr - wr_hi.astype(F32)).astype(BF16)
    br = jnp.zeros((1, HD), F32).at[0, :NEXP].set(b_router.astype(F32))
    return pl.pallas_call(
        _router_kernel,
        grid=(rows // TM,),
        in_specs=[pl.BlockSpec((TM, D), lambda i: (i, 0)),
                  pl.BlockSpec((1, D), lambda i: (0, 0)),
                  pl.BlockSpec((2, D, HD), lambda i: (0, 0, 0)),
                  pl.BlockSpec((1, HD), lambda i: (0, 0))],
        out_specs=[pl.BlockSpec((TM, D), lambda i: (i, 0)),
                   pl.BlockSpec((TM, HD), lambda i: (i, 0)),
                   pl.BlockSpec((TM, HD), lambda i: (i, 0))],
        out_shape=[jax.ShapeDtypeStruct((rows, D), F32),
                   jax.ShapeDtypeStruct((rows, HD), jnp.int32),
                   jax.ShapeDtypeStruct((rows, HD), F32)],
        compiler_params=_cparams(("parallel",)),
        name="moe_router",
    )(x, wn.reshape(1, D), jnp.stack([wr_hi, wr_lo]), br)


def _moe_kernel(te_ref, nv_ref, sa_ref, sr_ref, h_hbm, wa_ref, wb_ref, wd_ref, out_hbm,
                hbuf, hb16, obuf, gsem, ssem):
    del te_ref
    i = pl.program_id(0)
    j = pl.program_id(1)
    nt = pl.num_programs(0)
    last_j = pl.num_programs(1) - 1
    slot = lax.rem(i, 2)
    live = nv_ref[i] > 0

    def start_gather(tile, s):
        def body(g, carry):
            base = pl.multiple_of(g * 8, 8)
            for u in range(8):
                row = sr_ref[tile * MOE_TM + base + u]
                pltpu.make_async_copy(h_hbm.at[pl.ds(row, 1)], hbuf.at[s, pl.ds(base + u, 1)],
                                      gsem.at[s]).start(priority=u % 2)
            return carry
        lax.fori_loop(0, MOE_TM // 8, body, 0)

    def wait_gather(s):
        pltpu.make_async_copy(h_hbm.at[pl.ds(0, MOE_TM)], hbuf.at[s], gsem.at[s]).wait()

    def start_scatter(tile, s):
        def body(g, carry):
            base = pl.multiple_of(g * 8, 8)
            for u in range(8):
                a = sa_ref[tile * MOE_TM + base + u]
                pltpu.make_async_copy(obuf.at[s, pl.ds(base + u, 1)], out_hbm.at[pl.ds(a, 1)],
                                      ssem.at[s]).start(priority=u % 2)
            return carry
        lax.fori_loop(0, MOE_TM // 8, body, 0)

    def wait_scatter(s):
        pltpu.make_async_copy(obuf.at[s], out_hbm.at[pl.ds(0, MOE_TM)], ssem.at[s]).wait()

    @pl.when(j == 0)
    def _():
        @pl.when((i >= 2) & (nv_ref[jnp.maximum(i - 2, 0)] > 0))
        def _():
            wait_scatter(slot)

        @pl.when(live)
        def _():
            @pl.when(i == 0)
            def _():
                start_gather(0, 0)
                obuf[1] = jnp.zeros((MOE_TM, D), F32)
                for blk in range(2):
                    pltpu.make_async_copy(obuf.at[1], out_hbm.at[pl.ds(2 * R + blk * MOE_TM, MOE_TM)],
                                          ssem.at[1]).start()
                for blk in range(2):
                    wait_scatter(1)

            nxt = jnp.minimum(i + 1, nt - 1)

            @pl.when((i + 1 < nt) & (nv_ref[nxt] > 0))
            def _():
                start_gather(nxt, 1 - slot)

            wait_gather(slot)
            hb16[...] = hbuf[slot].astype(BF16)

    @pl.when(live)
    def _():
        h = hb16[...]
        a = jnp.dot(h, wa_ref[0], preferred_element_type=F32)
        b = jnp.dot(h, wb_ref[0], preferred_element_type=F32)
        act = (a * jax.nn.sigmoid(a) * b).astype(BF16)
        y = jnp.dot(act, wd_ref[0], preferred_element_type=F32)

        @pl.when(j == 0)
        def _():
            obuf[slot] = y

        @pl.when(j > 0)
        def _():
            obuf[slot] += y

        @pl.when(j == last_j)
        def _():
            start_scatter(i, slot)

    @pl.when((i == nt - 1) & (j == last_j))
    def _():
        @pl.when(nv_ref[nt - 2] > 0)
        def _():
            wait_scatter(1 - slot)

        @pl.when(live)
        def _():
            wait_scatter(slot)


def moe_experts(tile_expert, tile_rows, slot_assign, slot_row, h, w_up, w_down):
    nf = DFFE // MOE_FC
    grid_spec = pltpu.PrefetchScalarGridSpec(
        num_scalar_prefetch=4,
        grid=(MOE_NT, nf),
        in_specs=[pl.BlockSpec(memory_space=pl.ANY),
                  pl.BlockSpec((1, D, MOE_FC), lambda i, j, te, nv, sa, sr: (te[i], 0, j)),
                  pl.BlockSpec((1, D, MOE_FC), lambda i, j, te, nv, sa, sr: (te[i], 0, nf + j)),
                  pl.BlockSpec((1, MOE_FC, D), lambda i, j, te, nv, sa, sr: (te[i], j, 0))],
        out_specs=pl.BlockSpec(memory_space=pl.ANY),
        scratch_shapes=[pltpu.VMEM((2, MOE_TM, D), F32), pltpu.VMEM((MOE_TM, D), BF16),
                        pltpu.VMEM((2, MOE_TM, D), F32),
                        pltpu.SemaphoreType.DMA((2,)), pltpu.SemaphoreType.DMA((2,))],
    )
    return pl.pallas_call(
        _moe_kernel,
        grid_spec=grid_spec,
        out_shape=jax.ShapeDtypeStruct((2 * R + 2 * MOE_TM, D), F32),
        compiler_params=_cparams(("arbitrary", "arbitrary")),
        name="moe_experts",
    )(tile_expert, tile_rows, slot_assign, slot_row, h, w_up, w_up, w_down)


def _combine_kernel(x_ref, g_ref, e0_ref, e1_ref, wn_ref, y_ref):
    moe = g_ref[:, 0:1] * e0_ref[...] + g_ref[:, 1:2] * e1_ref[...]
    y_ref[...] = _rms(x_ref[...] + moe, wn_ref[...])


def moe_combine_norm(x, gates, eo, wn):
    rows = x.shape[0]
    nb = rows // TM
    return pl.pallas_call(
        _combine_kernel,
        grid=(rows // TM,),
        in_specs=[pl.BlockSpec((TM, D), lambda i: (i, 0)),
                  pl.BlockSpec((TM, HD), lambda i: (i, 0)),
                  pl.BlockSpec((TM, D), lambda i: (i, 0)),
                  pl.BlockSpec((TM, D), lambda i: (nb + i, 0)),
                  pl.BlockSpec((1, D), lambda i: (0, 0))],
        out_specs=pl.BlockSpec((TM, D), lambda i: (i, 0)),
        out_shape=jax.ShapeDtypeStruct((rows, D), F32),
        compiler_params=_cparams(("parallel",)),
        name="moe_combine_norm",
    )(x, gates, eo, eo, wn.reshape(1, D))


def _moe_plan(idx):
    e_flat = idx.reshape(-1)
    onehot = (e_flat[:, None] == jnp.arange(NEXP, dtype=jnp.int32)[None, :]).astype(jnp.int32)
    cum = jnp.cumsum(onehot, axis=0)
    counts = cum[-1]
    padded = ((counts + MOE_TM - 1) // MOE_TM) * MOE_TM
    ends = jnp.cumsum(padded)
    starts = ends - padded
    slot = jnp.sum(onehot * (starts[None, :] + cum - onehot), axis=1)
    pos = jnp.arange(MOE_NT * MOE_TM, dtype=jnp.int32)
    spare = 2 * R + ((pos // MOE_TM) % 2) * MOE_TM + pos % MOE_TM
    flat = jnp.arange(2 * R, dtype=jnp.int32)
    slot_assign = spare.at[slot].set((flat % 2) * R + flat // 2)
    tile_start = jnp.arange(MOE_NT, dtype=jnp.int32) * MOE_TM
    owner = (tile_start[:, None] >= starts[None, :]) & (tile_start[:, None] < ends[None, :])
    tile_rows = jnp.sum(jnp.where(owner, jnp.clip(counts[None, :] - (tile_start[:, None] - starts[None, :]),
                                                  0, MOE_TM), 0), axis=1).astype(jnp.int32)
    last_expert = jnp.max(jnp.where(counts > 0, jnp.arange(NEXP, dtype=jnp.int32), 0))
    tile_expert = jnp.where(jnp.any(owner, axis=1), jnp.argmax(owner, axis=1), last_expert).astype(jnp.int32)
    slot_row = jnp.where(slot_assign < 2 * R, slot_assign % R, 0).astype(jnp.int32)
    return tile_expert, tile_rows, slot_assign, slot_row


def kernel(x_prompt, x_sample, state_hgrn, cache_k, cache_v, page_table, meta_tokens,
           norm_mix_w, norm_ffn_w, hg_w_in, hg_lower_bound, hg_norm_w, hg_w_out,
           at_w_qkv, at_lambda_q1, at_lambda_k1, at_lambda_q2, at_lambda_k2, at_subln_w,
           at_w_out, rel_bias_table, ff_w_up, ff_w_down, moe_w_router, moe_b_router,
           moe_w_up, moe_w_down, final_norm_w):
    xp = jnp.concatenate([jnp.broadcast_to(meta_tokens.astype(F32), (B, NMETA, D)), x_prompt,
                          jnp.zeros((B, LP - LREAL, D), F32)], axis=1)
    x0 = jnp.concatenate([xp.reshape(RP, D), x_sample.reshape(RS, D)], axis=0)
    lower = jnp.cumsum(jax.nn.softmax(hg_lower_bound.astype(F32), axis=0), axis=0)

    w_in = hg_w_in[0].astype(BF16)
    y_p, st_p = hgrn_mixer_fused(x0, norm_mix_w[0], w_in, lower[0], hg_norm_w[0],
                                 B, HG_C, LP // HG_C, LREAL)
    proj_s = norm_matmul(x0[RP:], norm_mix_w[0], w_in, 1024)
    proj_s = jnp.pad(proj_s.reshape(DB, DT, 4 * D), ((0, 0), (0, HG_CS - DT), (0, 0)))
    y_s, st_s = hgrn_mixer(proj_s.reshape(DB * HG_CS, 4 * D), lower[0], hg_norm_w[0], state_hgrn[0],
                           DB, HG_CS, 1, DT, True, n_seq=HG_NS)
    x1 = matmul_residual(y_p, y_s.reshape(DB, HG_CS, D)[:, :DT].reshape(RS, D),
                         hg_w_out[0].astype(BF16), x0)
    x2 = dense_ffn(x1, norm_ffn_w[0], ff_w_up[0].astype(BF16), ff_w_down[0].astype(BF16))

    f = F32
    lam = (jnp.exp(jnp.sum(at_lambda_q1[0].astype(f) * at_lambda_k1[0].astype(f)))
           - jnp.exp(jnp.sum(at_lambda_q2[0].astype(f) * at_lambda_k2[0].astype(f))) + LAMBDA_INIT)
    lam = lam.reshape(1).astype(F32)
    qkv = norm_matmul(x2, norm_mix_w[1], at_w_qkv[0].astype(BF16), 1024)
    q_all, k_all, v_all = qkv[:, :D], qkv[:, D:2 * D], qkv[:, 2 * D:]
    qscale = DQK ** -0.5

    T = AT_T
    dvals = np.arange(-(T - 1), T)
    f_diag = jnp.where(jnp.asarray(dvals >= 0)[None, :], _bias_by_distance(dvals, rel_bias_table), NEG)
    f_prev = _bias_by_distance(dvals + T, rel_bias_table)
    cfar = rel_bias_table[NBUCKETS - 1].astype(F32)
    strips = jnp.stack([_toeplitz(f_diag, T), _toeplitz(f_prev, T),
                        jnp.broadcast_to(cfar[:, None, None], (H, T, T))], axis=1)
    o_p = attn_prompt(qkv, strips, lam, at_subln_w[0])

    q8 = jnp.pad((q_all[RP:] * qscale).reshape(DB, DT, D), ((0, 0), (0, 8 - DT), (0, 0)))
    tq = np.minimum(np.arange(8), DT - 1)
    kpos = np.arange(PAST).reshape(NPAGES, 1, PAGE)
    bp = _bias_by_distance(PAST + tq[None, :, None] - kpos, rel_bias_table)
    bias_pages = jnp.broadcast_to(bp.transpose(1, 0, 2, 3)[:, :, None], (NPAGES, H, 2, 8, PAGE))
    bias_pages = bias_pages.reshape(NPAGES // PP, PP, HD, PAGE).transpose(0, 2, 1, 3)
    bias_pages = bias_pages.reshape(NPAGES // PP, HD, PP * PAGE)
    sn = np.arange(16)
    dn = tq[:, None] - sn[None, :]
    bn = jnp.where(jnp.asarray((dn >= 0) & (sn[None, :] < DT))[None],
                   _bias_by_distance(dn, rel_bias_table), NEG)
    bias_new = jnp.broadcast_to(bn[:, None], (H, 2, 8, 16)).reshape(HD, 16)
    k_new = jnp.pad(k_all[RP:].reshape(DB, DT, D), ((0, 0), (0, 16 - DT), (0, 0)))
    v_new = jnp.pad(v_all[RP:].reshape(DB, DT, D), ((0, 0), (0, 16 - DT), (0, 0)))
    n_phys = cache_k.shape[1]
    cache_kt = jnp.transpose(cache_k[0], (0, 2, 3, 4, 1)).reshape(n_phys * D, PAGE)
    cache_v2 = cache_v[0].reshape(n_phys * PAGE * H, HD)
    o_s = attn_decode(page_table, lam, q8, cache_kt, cache_v2, bias_pages, k_new, v_new, bias_new,
                      at_subln_w[0])
    x3 = matmul_residual(o_p, o_s[:, :DT].reshape(RS, D), at_w_out[0].astype(BF16), x2)

    h_moe, idx, gates = moe_router(x3, norm_ffn_w[1], moe_w_router[0], moe_b_router[0])
    tile_expert, tile_rows, slot_assign, slot_row = _moe_plan(idx[:, :TOPK])
    eo = moe_experts(tile_expert, tile_rows, slot_assign, slot_row, h_moe,
                     moe_w_up[0].astype(BF16), moe_w_down[0].astype(BF16))
    y = moe_combine_norm(x3, gates, eo, final_norm_w)

    y_prompt = y[:RP].reshape(B, LP, D)[:, NMETA:LREAL]
    y_sample = y[RP:].reshape(DB, DT, D)
    k_p = k_all[:RP].reshape(B, LP, H, 2, DQK)[:, :LREAL]
    v_p = v_all[:RP].reshape(B, LP, H, HD)[:, :LREAL]
    k_s = k_all[RP:].reshape(DB, DT, H, 2, DQK)
    v_s = v_all[RP:].reshape(DB, DT, H, HD)
    return (y_prompt, y_sample, st_p[None], st_s[None], k_p[None], v_p[None], k_s[None], v_s[None])
```

```python
import functools
import math

import numpy as np
import jax
import jax.numpy as jnp
from jax import lax
from jax.experimental import pallas as pl
from jax.experimental.pallas import tpu as pltpu

F32 = jnp.float32
BF16 = jnp.bfloat16

D = 1024
B = 8
SEQ = 2048
DB = 128
DT = 4
PAST = 2048
PAGE = 128
NPAGES = PAST // PAGE
NMETA = 16
LREAL = NMETA + SEQ
LP = 2176
RP = B * LP
RS = DB * DT
R = RP + RS
H = 8
HD = 128
DQK = 64
NBUCKETS = 32
MAXDIST = 128
DFF = 2816
NEXP = 8
TOPK = 2
DFFE = 3584
EPS = 1e-6
NEG = -1e30
LAMBDA_INIT = 0.8 - 0.6 * math.exp(-0.3 * 1)

VMEM_LIMIT = 56 * 1024 * 1024

TM = 512
HG_C = 128
HG_CS = 16
HG_NS = 4
AT_T = 256
MOE_TM = 512
MOE_NT = (2 * R) // MOE_TM + NEXP
MOE_FC = 1792
FFN_FC = 1408
PP = 8


def _cparams(sem, vmem=VMEM_LIMIT):
    return pltpu.CompilerParams(dimension_semantics=sem, vmem_limit_bytes=vmem)


def _rms(x, w):
    return (x * lax.rsqrt(jnp.mean(x * x, axis=-1, keepdims=True) + EPS)) * w


def _norm_matmul_kernel(x_ref, wn_ref, w_ref, o_ref, h_ref):
    @pl.when(pl.program_id(1) == 0)
    def _():
        h_ref[...] = _rms(x_ref[...], wn_ref[...]).astype(BF16)

    o_ref[...] = jnp.dot(h_ref[...], w_ref[...], preferred_element_type=F32)


def norm_matmul(x, wn, w, tn):
    rows, n = x.shape[0], w.shape[1]
    return pl.pallas_call(
        _norm_matmul_kernel,
        grid=(rows // TM, n // tn),
        in_specs=[pl.BlockSpec((TM, D), lambda i, j: (i, 0)),
                  pl.BlockSpec((1, D), lambda i, j: (0, 0)),
                  pl.BlockSpec((D, tn), lambda i, j: (0, j))],
        out_specs=pl.BlockSpec((TM, tn), lambda i, j: (i, j)),
        out_shape=jax.ShapeDtypeStruct((rows, n), F32),
        scratch_shapes=[pltpu.VMEM((TM, D), BF16)],
        compiler_params=_cparams(("parallel", "arbitrary")),
        name="norm_matmul",
    )(x, wn.reshape(1, D), w)


def _matmul_residual_kernel(ap_ref, as_ref, w_ref, xp_ref, xs_ref, o_ref, *, n_prompt_tiles, one_x):
    def emit(a_ref, x_ref):
        o_ref[...] = x_ref[...] + jnp.dot(a_ref[...].astype(BF16), w_ref[...],
                                          preferred_element_type=F32)

    i = pl.program_id(0)
    pl.when(i < n_prompt_tiles)(lambda: emit(ap_ref, xp_ref))
    pl.when(i >= n_prompt_tiles)(lambda: emit(as_ref, xp_ref if one_x else xs_ref))


def matmul_residual(a_prompt, a_decode, w, x_prompt, x_decode):
    n_p = a_prompt.shape[0] // TM
    n_s = a_decode.shape[0] // TM
    prompt_map = lambda i: (jnp.minimum(i, n_p - 1), 0)
    decode_map = lambda i: (jnp.maximum(i - n_p, 0), 0)
    one_x = x_decode is None
    if one_x:
        x_decode, xp_map, xs_map = x_prompt, (lambda i: (i, 0)), (lambda i: (0, 0))
    else:
        xp_map, xs_map = prompt_map, decode_map
    return pl.pallas_call(
        functools.partial(_matmul_residual_kernel, n_prompt_tiles=n_p, one_x=one_x),
        grid=(n_p + n_s,),
        in_specs=[pl.BlockSpec((TM, D), prompt_map),
                  pl.BlockSpec((TM, D), decode_map),
                  pl.BlockSpec((D, D), lambda i: (0, 0)),
                  pl.BlockSpec((TM, D), xp_map),
                  pl.BlockSpec((TM, D), xs_map)],
        out_specs=pl.BlockSpec((TM, D), lambda i: (i, 0)),
        out_shape=jax.ShapeDtypeStruct(((n_p + n_s) * TM, D), F32),
        compiler_params=_cparams(("parallel",)),
        name="matmul_residual",
    )(a_prompt, a_decode, w, x_prompt, x_decode)


def _ffn_kernel(x_ref, wn_ref, wa_ref, wb_ref, wd_ref, o_ref, h_ref):
    j = pl.program_id(1)

    @pl.when(j == 0)
    def _():
        h_ref[...] = _rms(x_ref[...], wn_ref[...]).astype(BF16)

    h = h_ref[...]
    a = jnp.dot(h, wa_ref[...], preferred_element_type=F32)
    b = jnp.dot(h, wb_ref[...], preferred_element_type=F32)
    act = (a * jax.nn.sigmoid(a) * b).astype(BF16)
    y = jnp.dot(act, wd_ref[...], preferred_element_type=F32)

    @pl.when(j == 0)
    def _():
        o_ref[...] = x_ref[...] + y

    @pl.when(j > 0)
    def _():
        o_ref[...] += y


def dense_ffn(x, wn, w_up, w_down):
    rows = x.shape[0]
    nf = DFF // FFN_FC
    return pl.pallas_call(
        _ffn_kernel,
        grid=(rows // TM, nf),
        in_specs=[pl.BlockSpec((TM, D), lambda i, j: (i, 0)),
                  pl.BlockSpec((1, D), lambda i, j: (0, 0)),
                  pl.BlockSpec((D, FFN_FC), lambda i, j: (0, j)),
                  pl.BlockSpec((D, FFN_FC), lambda i, j: (0, nf + j)),
                  pl.BlockSpec((FFN_FC, D), lambda i, j: (j, 0))],
        out_specs=pl.BlockSpec((TM, D), lambda i, j: (i, 0)),
        out_shape=jax.ShapeDtypeStruct((rows, D), F32),
        scratch_shapes=[pltpu.VMEM((TM, D), BF16)],
        compiler_params=_cparams(("parallel", "arbitrary")),
        name="dense_ffn",
    )(x, wn.reshape(1, D), w_up, w_up, w_down)


def _hgrn_masks(C):
    nlev = int(math.log2(C))
    t = np.arange(C)[:, None]
    s = np.arange(C)[None, :]
    m = [(t == s)]
    for lev in range(nlev):
        b = 1 << lev
        m.append((t // (2 * b) == s // (2 * b)) & ((t // b) % 2 == 1) & ((s // b) % 2 == 0))
    return np.stack(m).astype(np.float32)


def _hgrn_kernel(proj_ref, lb_ref, nw_ref, msk_ref, s0_ref, y_ref, s_ref, *, C, n_valid, n_seq):
    for bb in range(n_seq):
        _hgrn_chunk(proj_ref.at[pl.ds(bb * C, C)], lb_ref, nw_ref, msk_ref, s0_ref.at[pl.ds(bb, 1)],
                    y_ref.at[pl.ds(bb * C, C)], s_ref.at[pl.ds(bb, 1)], C, n_valid)


def _hgrn_chunk(proj_ref, lb_ref, nw_ref, msk_ref, s0_ref, y_ref, s_ref, C, n_valid):
    nlev = int(math.log2(C))
    c = pl.program_id(1)

    @pl.when(c == 0)
    def _():
        s_ref[...] = s0_ref[...]

    rloc = lax.broadcasted_iota(jnp.int32, (C, HD), 0)
    valid = (c * C + rloc) < n_valid
    nt = (((1,), (1,)), ((), ()))
    tn = (((0,), (0,)), ((), ()))
    o_heads = []
    for h in range(H):
        col = slice(h * HD, (h + 1) * HD)
        lb = lb_ref[:, col]
        q = proj_ref[:, h * HD:(h + 1) * HD]
        f = proj_ref[:, D + h * HD:D + (h + 1) * HD]
        v = proj_ref[:, 2 * D + h * HD:2 * D + (h + 1) * HD]
        qs = q * jax.nn.sigmoid(q)
        fg = lb + (1.0 - lb) * jax.nn.sigmoid(f)
        k = jnp.where(valid, 1.0 - fg, 0.0)
        g = jnp.where(valid, jnp.log2(fg), 0.0)
        G = g
        sh = 1
        while sh < C:
            G = G + jnp.where(rloc >= sh, pltpu.roll(G, sh, 0), 0.0)
            sh *= 2
        vb = v.astype(BF16)
        A = msk_ref[0] * lax.dot_general(qs.astype(BF16), k.astype(BF16), nt,
                                         preferred_element_type=F32)
        for lev in range(nlev):
            b = 1 << lev
            if b >= 8:
                gm = G.reshape(C // (2 * b), 2 * b, HD)[:, b - 1:b, :]
                gmid = jnp.broadcast_to(gm, (C // (2 * b), 2 * b, HD)).reshape(C, HD)
            else:
                g8 = G.reshape(C // 8, 8, HD)
                sub = lax.broadcasted_iota(jnp.int32, (C // 8, 8, HD), 1)
                gmid = None
                for base in range(0, 8, 2 * b):
                    row = jnp.broadcast_to(g8[:, base + b - 1:base + b, :], (C // 8, 8, HD))
                    gmid = row if gmid is None else jnp.where(sub >= base, row, gmid)
                gmid = gmid.reshape(C, HD)
            right = (rloc // b) % 2 == 1
            fac = jnp.exp2(jnp.where(right, G - gmid, gmid - G))
            qt = jnp.where(right, qs * fac, 0.0).astype(BF16)
            kt = jnp.where(right, 0.0, k * fac).astype(BF16)
            A = A + msk_ref[lev + 1] * lax.dot_general(qt, kt, nt, preferred_element_type=F32)
        g_end = G[C - 1:C, :]
        qh = (qs * jnp.exp2(G)).astype(BF16)
        kh = (k * jnp.exp2(g_end - G)).astype(BF16)
        S = s_ref[0, h]
        o_h = (jnp.dot(A.astype(BF16), vb, preferred_element_type=F32)
               + jnp.dot(qh, S.astype(BF16), preferred_element_type=F32))
        e_col = jnp.broadcast_to(jnp.exp2(g_end), (HD, HD)).T
        s_ref[0, h] = e_col * S + lax.dot_general(kh, vb, tn, preferred_element_type=F32)
        o_heads.append(o_h)
    o = jnp.concatenate(o_heads, axis=1)
    gate = proj_ref[:, 3 * D:4 * D]
    y_ref[...] = _rms(o * jax.nn.sigmoid(gate), nw_ref[...]).astype(BF16)


def _hgrn_fused_kernel(x_ref, xn_ref, wn_ref, w_ref, lb_ref, nw_ref, msk_ref, s0_ref, y_ref, s_ref,
                       proj_scr, *, C, n_valid):
    c = pl.program_id(1)
    slot = lax.rem(c, 2)

    def project(x):
        return jnp.dot(_rms(x, wn_ref[...]).astype(BF16), w_ref[...], preferred_element_type=F32)

    @pl.when(c == 0)
    def _():
        proj_scr[0] = project(x_ref[...])

    proj_scr[1 - slot] = project(xn_ref[...])
    _hgrn_chunk(proj_scr.at[slot], lb_ref, nw_ref, msk_ref, s0_ref, y_ref, s_ref, C, n_valid)


def hgrn_mixer_fused(x, wn, w_in, lb, nw, nb, C, n_chunks, n_valid):
    msk = jnp.asarray(_hgrn_masks(C))
    nlev1 = msk.shape[0]
    const2 = lambda b, c: (0, 0)
    return pl.pallas_call(
        functools.partial(_hgrn_fused_kernel, C=C, n_valid=n_valid),
        grid=(nb, n_chunks),
        in_specs=[pl.BlockSpec((C, D), lambda b, c: (b * n_chunks + c, 0)),
                  pl.BlockSpec((C, D), lambda b, c: (b * n_chunks + jnp.minimum(c + 1, n_chunks - 1), 0)),
                  pl.BlockSpec((1, D), const2),
                  pl.BlockSpec((D, 4 * D), const2),
                  pl.BlockSpec((1, D), const2),
                  pl.BlockSpec((1, D), const2),
                  pl.BlockSpec((nlev1, C, C), lambda b, c: (0, 0, 0)),
                  pl.BlockSpec((1, H, HD, HD), lambda b, c: (0, 0, 0, 0))],
        out_specs=[pl.BlockSpec((C, D), lambda b, c: (b * n_chunks + c, 0)),
                   pl.BlockSpec((1, H, HD, HD), lambda b, c: (b, 0, 0, 0))],
        out_shape=[jax.ShapeDtypeStruct((nb * n_chunks * C, D), BF16),
                   jax.ShapeDtypeStruct((nb, H, HD, HD), F32)],
        scratch_shapes=[pltpu.VMEM((2, C, 4 * D), F32)],
        compiler_params=_cparams(("parallel", "arbitrary")),
        name="hgrn_mixer_fused",
    )(x, x, wn.reshape(1, D), w_in, lb.reshape(1, D), nw.reshape(1, D), msk,
      jnp.zeros((1, H, HD, HD), F32))


def hgrn_mixer(proj, lb, nw, s0, nb, C, n_chunks, n_valid, s0_per_batch, n_seq=1):
    assert n_seq == 1 or (n_chunks == 1 and s0_per_batch and nb % n_seq == 0)
    msk = jnp.asarray(_hgrn_masks(C))
    nlev1 = msk.shape[0]
    s0_map = (lambda b, c: (b, 0, 0, 0)) if s0_per_batch else (lambda b, c: (0, 0, 0, 0))
    return pl.pallas_call(
        functools.partial(_hgrn_kernel, C=C, n_valid=n_valid, n_seq=n_seq),
        grid=(nb // n_seq, n_chunks),
        in_specs=[pl.BlockSpec((n_seq * C, 4 * D), lambda b, c: (b * n_chunks + c, 0)),
                  pl.BlockSpec((1, D), lambda b, c: (0, 0)),
                  pl.BlockSpec((1, D), lambda b, c: (0, 0)),
                  pl.BlockSpec((nlev1, C, C), lambda b, c: (0, 0, 0)),
                  pl.BlockSpec((n_seq, H, HD, HD), s0_map)],
        out_specs=[pl.BlockSpec((n_seq * C, D), lambda b, c: (b * n_chunks + c, 0)),
                   pl.BlockSpec((n_seq, H, HD, HD), lambda b, c: (b, 0, 0, 0))],
        out_shape=[jax.ShapeDtypeStruct((nb * n_chunks * C, D), BF16),
                   jax.ShapeDtypeStruct((nb, H, HD, HD), F32)],
        compiler_params=_cparams(("parallel", "arbitrary")),
        name="hgrn_mixer",
    )(proj, lb.reshape(1, D), nw.reshape(1, D), msk, s0)


def _bucket_of_distance(dist):
    n = np.maximum(np.asarray(dist), 0)
    max_exact = NBUCKETS // 2
    nf = np.maximum(n, 1).astype(np.float64)
    large = max_exact + (np.log(nf / max_exact) / math.log(MAXDIST / max_exact)
                         * (NBUCKETS - max_exact)).astype(np.int64)
    return np.where(n < max_exact, n, np.minimum(large, NBUCKETS - 1))


def _bias_by_distance(dist, table):
    dist = np.asarray(dist)
    onehot = np.eye(NBUCKETS, dtype=np.float32)[_bucket_of_distance(dist).reshape(-1)]
    out = jnp.einsum("nb,bh->hn", jnp.asarray(onehot), table.astype(F32),
                     precision=lax.Precision.HIGHEST)
    return out.reshape((H,) + dist.shape)


def _toeplitz(fvals, T):
    w = jnp.pad(fvals[:, ::-1], ((0, 0), (0, 1)))
    flat = jnp.tile(w, (1, T))[:, :T * (2 * T - 1)]
    return flat.reshape(H, T, 2 * T - 1)[:, :, T - 1:]


def _subln(d, w):
    return _rms(d, w) * (1.0 - LAMBDA_INIT)


def _attn_prompt_kernel(lam_ref, q_ref, k_ref, v_ref, strip_ref, sw_ref, o_ref, s_scr, m_scr, kb_scr, vx_scr):
    T = AT_T
    starts = list(range(0, LP, T))
    sizes = [min(T, LP - r0) for r0 in starts]
    nq = len(starts)
    nt = (((1,), (1,)), ((), ()))
    kb_scr[...] = k_ref[...].astype(BF16)
    vx_scr[:, :HD] = v_ref[...].astype(BF16)
    vx_scr[:, HD:] = jnp.ones((LP, HD), BF16)

    def step(t):
        slot = t % 2
        if t < nq:
            tq = sizes[t]
            qb = (q_ref[starts[t]:starts[t] + tq, :] * (DQK ** -0.5)).astype(BF16)
            lane = lax.broadcasted_iota(jnp.int32, (tq, HD), 1)
            zero = jnp.zeros_like(qb)
            qq = jnp.concatenate([jnp.where(lane < DQK, qb, zero), jnp.where(lane >= DQK, qb, zero)], axis=0)
            mx = jnp.full((2 * tq, HD), NEG, F32)
        if t > 0:
            tp = sizes[t - 1]
            m_prev = m_scr[1 - slot, :2 * tp, :]
            acc = jnp.zeros((2 * tp, 2 * HD), F32)
        for c in range(min(t + 1, nq)):
            tk = sizes[c]
            if t < nq:
                bt = strip_ref[0, min(t - c, 2), :tq, :tk]
                sc = lax.dot_general(qq, kb_scr[starts[c]:starts[c] + tk, :], nt,
                                     preferred_element_type=F32) + jnp.concatenate([bt, bt], axis=0)
                s_scr[slot, c, :2 * tq, :tk] = sc
                for u in range(tk // HD):
                    mx = jnp.maximum(mx, sc[:, u * HD:(u + 1) * HD])
            if t > 0 and c < t:
                sp = s_scr[1 - slot, c, :2 * tp, :tk]
                p = jnp.exp(jnp.concatenate([sp[:, u * HD:(u + 1) * HD] - m_prev for u in range(tk // HD)],
                                            axis=1))
                acc = acc + jnp.dot(p.astype(BF16), vx_scr[starts[c]:starts[c] + tk, :],
                                    preferred_element_type=F32)
        if t < nq:
            m_scr[slot, :2 * tq, :] = jnp.broadcast_to(jnp.max(mx, axis=-1, keepdims=True), (2 * tq, HD))
        if t > 0:
            o = acc[:, :HD] / acc[:, HD:]
            o_ref[starts[t - 1]:starts[t - 1] + tp, :] = _subln(o[:tp] - lam_ref[0] * o[tp:], sw_ref[...])

    for t in range(nq + 1):
        step(t)


def attn_prompt(qkv, strips, lam, sw):
    T = AT_T
    nq = -(-LP // T)
    smem = pl.BlockSpec(memory_space=pltpu.SMEM)
    return pl.pallas_call(
        _attn_prompt_kernel,
        grid=(B, H),
        in_specs=[smem,
                  pl.BlockSpec((LP, HD), lambda b, h: (b, h)),
                  pl.BlockSpec((LP, HD), lambda b, h: (b, H + h)),
                  pl.BlockSpec((LP, HD), lambda b, h: (b, 2 * H + h)),
                  pl.BlockSpec((1, 3, T, T), lambda b, h: (h, 0, 0, 0)),
                  pl.BlockSpec((1, HD), lambda b, h: (0, 0))],
        out_specs=pl.BlockSpec((LP, HD), lambda b, h: (b, h)),
        out_shape=jax.ShapeDtypeStruct((B * LP, D), F32),
        scratch_shapes=[pltpu.VMEM((2, nq, 2 * T, T), F32), pltpu.VMEM((2, 2 * T, HD), F32),
                        pltpu.VMEM((LP, HD), BF16), pltpu.VMEM((LP, 2 * HD), BF16)],
        compiler_params=_cparams(("parallel", "parallel")),
        name="attn_prompt",
    )(lam, qkv, qkv, qkv, strips, sw.reshape(1, HD))


def _attn_decode_kernel(pt_ref, lam_ref, q_ref, *refs):
    del pt_ref
    k_refs = refs[:PP]
    v_refs = refs[PP:2 * PP]
    bias_ref, kn_ref, vn_ref, bn_ref, sw_ref, o_ref, m_ref, acc_ref, q_scr = refs[2 * PP:]
    g = pl.program_id(1)
    last = pl.num_programs(1) - 1
    nt = (((1,), (1,)), ((), ()))

    @pl.when(g == 0)
    def _():
        m_ref[...] = jnp.full(m_ref.shape, NEG, F32)
        acc_ref[...] = jnp.zeros(acc_ref.shape, F32)
        q8 = q_ref[0]
        r = lax.broadcasted_iota(jnp.int32, (HD, D), 0)
        col = lax.broadcasted_iota(jnp.int32, (HD, D), 1)
        q_scr[...] = jnp.where(r // 8 == col // DQK, jnp.concatenate([q8] * 16, axis=0), 0.0).astype(BF16)

    qrows = q_scr[...]
    kt = jnp.concatenate([k_refs[j][...].astype(BF16) for j in range(PP)], axis=1)
    s = jnp.dot(qrows, kt, preferred_element_type=F32) + bias_ref[0]
    ones = jnp.ones((PP * PAGE, HD), BF16)

    def finish(s_new):
        smax = s[:, :PAGE]
        for j in range(1, PP):
            smax = jnp.maximum(smax, s[:, j * PAGE:(j + 1) * PAGE])
        m_cur = jnp.max(smax, axis=-1, keepdims=True)
        if s_new is not None:
            m_cur = jnp.maximum(m_cur, jnp.max(s_new, axis=-1, keepdims=True))
        m_old = m_ref[...]
        m_new = jnp.maximum(m_old, jnp.broadcast_to(m_cur, (HD, HD)))
        m_ref[...] = m_new
        alpha = jnp.exp(m_old - m_new)
        alpha2 = jnp.concatenate([alpha, alpha], axis=1)
        p = jnp.exp(s - jnp.concatenate([m_new] * PP, axis=1)).astype(BF16)
        if s_new is not None:
            p_new = jnp.exp(s_new - m_new[:, :16]).astype(BF16)
        for hh in range(H):
            rows = slice(hh * 16, (hh + 1) * 16)
            vh = jnp.concatenate([v_refs[j][pl.ds(hh, PAGE, stride=H), :].astype(BF16)
                                  for j in range(PP)], axis=0)
            pv = jnp.dot(p[rows], jnp.concatenate([vh, ones], axis=1), preferred_element_type=F32)
            if s_new is not None:
                vn = vn_ref[0, :, hh * HD:(hh + 1) * HD].astype(BF16)
                pv = pv + jnp.dot(p_new[rows], jnp.concatenate([vn, ones[:16]], axis=1),
                                  preferred_element_type=F32)
            acc_ref[rows, :] = alpha2[rows] * acc_ref[rows, :] + pv

    @pl.when(g < last)
    def _():
        finish(None)

    @pl.when(g == last)
    def _():
        finish(lax.dot_general(qrows, kn_ref[0].astype(BF16), nt,
                               preferred_element_type=F32) + bn_ref[...])
        acc = acc_ref[...]
        o = acc[:, :HD] / acc[:, HD:]
        outs = []
        for hh in range(H):
            o1 = o[hh * 16:hh * 16 + 8]
            o2 = o[hh * 16 + 8:hh * 16 + 16]
            outs.append(_subln(o1 - lam_ref[0] * o2, sw_ref[...]))
        o_ref[0] = jnp.concatenate(outs, axis=1)


def attn_decode(page_table, lam, q8, cache_kt, cache_v2, bias_pages, k_new, v_new, bias_new, sw):
    n_steps = NPAGES // PP

    def page_spec(j):
        return pl.BlockSpec((D, PAGE), lambda b, g, pt: (pt[b * NPAGES + g * PP + j], 0))

    smem = pl.BlockSpec(memory_space=pltpu.SMEM)
    grid_spec = pltpu.PrefetchScalarGridSpec(
        num_scalar_prefetch=1,
        grid=(DB, n_steps),
        in_specs=([smem, pl.BlockSpec((1, 8, D), lambda b, g, pt: (b, 0, 0))]
                  + [page_spec(j) for j in range(PP)]
                  + [page_spec(j) for j in range(PP)]
                  + [pl.BlockSpec((1, HD, PP * PAGE), lambda b, g, pt: (g, 0, 0)),
                     pl.BlockSpec((1, 16, D), lambda b, g, pt: (b, 0, 0)),
                     pl.BlockSpec((1, 16, D), lambda b, g, pt: (b, 0, 0)),
                     pl.BlockSpec((HD, 16), lambda b, g, pt: (0, 0)),
                     pl.BlockSpec((1, HD), lambda b, g, pt: (0, 0))]),
        out_specs=pl.BlockSpec((1, 8, D), lambda b, g, pt: (b, 0, 0)),
        scratch_shapes=[pltpu.VMEM((HD, HD), F32), pltpu.VMEM((HD, 2 * HD), F32),
                        pltpu.VMEM((HD, D), BF16)],
    )
    return pl.pallas_call(
        _attn_decode_kernel,
        grid_spec=grid_spec,
        out_shape=jax.ShapeDtypeStruct((DB, 8, D), F32),
        compiler_params=_cparams(("parallel", "arbitrary")),
        name="attn_decode",
    )(page_table.reshape(-1), lam, q8, *([cache_kt] * PP), *([cache_v2] * PP),
      bias_pages, k_new, v_new, bias_new, sw.reshape(1, HD))


def _router_kernel(x_ref, wn_ref, wr_ref, br_ref, h_ref, idx_ref, gate_ref):
    h = _rms(x_ref[...], wn_ref[...])
    h_ref[...] = h
    h_hi = h.astype(BF16)
    h_lo = (h - h_hi.astype(F32)).astype(BF16)
    logits = (jnp.dot(h_hi, wr_ref[0], preferred_element_type=F32)
              + jnp.dot(h_lo, wr_ref[0], preferred_element_type=F32)
              + jnp.dot(h_hi, wr_ref[1], preferred_element_type=F32)) + br_ref[...]
    lane = lax.broadcasted_iota(jnp.int32, logits.shape, 1)
    ninf = jnp.float32(-jnp.inf)
    logits = jnp.where(lane < NEXP, logits, ninf)
    v1 = jnp.max(logits, axis=-1, keepdims=True)
    i1 = jnp.min(jnp.where(logits == v1, lane, HD), axis=-1, keepdims=True)
    rest = jnp.where(lane == i1, ninf, logits)
    v2 = jnp.max(rest, axis=-1, keepdims=True)
    i2 = jnp.min(jnp.where(rest == v2, lane, HD), axis=-1, keepdims=True)
    e = jnp.exp(v2 - v1)
    g1 = 1.0 / (1.0 + e)
    g2 = e / (1.0 + e)
    idx_ref[...] = jnp.where(lane == 0, i1, jnp.where(lane == 1, i2, 0))
    gate_ref[...] = jnp.where(lane == 0, g1, jnp.where(lane == 1, g2, 0.0))


def moe_router(x, wn, w_router, b_router):
    rows = x.shape[0]
    wr = jnp.zeros((D, HD), F32).at[:, :NEXP].set(w_router.astype(F32))
    wr_hi = wr.astype(BF16)
    wr_lo = (wr - wr_hi.astype(F32)).astype(BF16)
    br = jnp.zeros((1, HD), F32).at[0, :NEXP].set(b_router.astype(F32))
    return pl.pallas_call(
        _router_kernel,
        grid=(rows // TM,),
        in_specs=[pl.BlockSpec((TM, D), lambda i: (i, 0)),
                  pl.BlockSpec((1, D), lambda i: (0, 0)),
                  pl.BlockSpec((2, D, HD), lambda i: (0, 0, 0)),
                  pl.BlockSpec((1, HD), lambda i: (0, 0))],
        out_specs=[pl.BlockSpec((TM, D), lambda i: (i, 0)),
                   pl.BlockSpec((TM, HD), lambda i: (i, 0)),
                   pl.BlockSpec((TM, HD), lambda i: (i, 0))],
        out_shape=[jax.ShapeDtypeStruct((rows, D), F32),
                   jax.ShapeDtypeStruct((rows, HD), jnp.int32),
                   jax.ShapeDtypeStruct((rows, HD), F32)],
        compiler_params=_cparams(("parallel",)),
        name="moe_router",
    )(x, wn.reshape(1, D), jnp.stack([wr_hi, wr_lo]), br)


def _moe_kernel(te_ref, nv_ref, sa_ref, sr_ref, h_hbm, wa_ref, wb_ref, wd_ref, out_hbm,
                hbuf, hb16, obuf, gsem, ssem):
    del te_ref
    i = pl.program_id(0)
    j = pl.program_id(1)
    nt = pl.num_programs(0)
    last_j = pl.num_programs(1) - 1
    slot = lax.rem(i, 2)
    live = nv_ref[i] > 0

    def start_gather(tile, s):
        def body(g, carry):
            base = pl.multiple_of(g * 8, 8)
            for u in range(8):
                row = sr_ref[tile * MOE_TM + base + u]
                pltpu.make_async_copy(h_hbm.at[pl.ds(row, 1)], hbuf.at[s, pl.ds(base + u, 1)],
                                      gsem.at[s]).start()
            return carry
        lax.fori_loop(0, MOE_TM // 8, body, 0)

    def wait_gather(s):
        pltpu.make_async_copy(h_hbm.at[pl.ds(0, MOE_TM)], hbuf.at[s], gsem.at[s]).wait()

    def start_scatter(tile, s):
        def body(g, carry):
            base = pl.multiple_of(g * 8, 8)
            for u in range(8):
                a = sa_ref[tile * MOE_TM + base + u]
                pltpu.make_async_copy(obuf.at[s, pl.ds(base + u, 1)], out_hbm.at[pl.ds(a, 1)],
                                      ssem.at[s]).start()
            return carry
        lax.fori_loop(0, MOE_TM // 8, body, 0)

    def wait_scatter(s):
        pltpu.make_async_copy(obuf.at[s], out_hbm.at[pl.ds(0, MOE_TM)], ssem.at[s]).wait()

    @pl.when(j == 0)
    def _():
        @pl.when((i >= 2) & (nv_ref[jnp.maximum(i - 2, 0)] > 0))
        def _():
            wait_scatter(slot)

        @pl.when(live)
        def _():
            @pl.when(i == 0)
            def _():
                start_gather(0, 0)
                obuf[1] = jnp.zeros((MOE_TM, D), F32)
                for blk in range(2):
                    pltpu.make_async_copy(obuf.at[1], out_hbm.at[pl.ds(2 * R + blk * MOE_TM, MOE_TM)],
                                          ssem.at[1]).start()
                for blk in range(2):
                    wait_scatter(1)

            nxt = jnp.minimum(i + 1, nt - 1)

            @pl.when((i + 1 < nt) & (nv_ref[nxt] > 0))
            def _():
                start_gather(nxt, 1 - slot)

            wait_gather(slot)
            hb16[...] = hbuf[slot].astype(BF16)

    @pl.when(live)
    def _():
        h = hb16[...]
        a = jnp.dot(h, wa_ref[0], preferred_element_type=F32)
        b = jnp.dot(h, wb_ref[0], preferred_element_type=F32)
        act = (a * jax.nn.sigmoid(a) * b).astype(BF16)
        y = jnp.dot(act, wd_ref[0], preferred_element_type=F32)

        @pl.when(j == 0)
        def _():
            obuf[slot] = y

        @pl.when(j > 0)
        def _():
            obuf[slot] += y

        @pl.when(j == last_j)
        def _():
            start_scatter(i, slot)

    @pl.when((i == nt - 1) & (j == last_j))
    def _():
        @pl.when(nv_ref[nt - 2] > 0)
        def _():
            wait_scatter(1 - slot)

        @pl.when(live)
        def _():
            wait_scatter(slot)


def moe_experts(tile_expert, tile_rows, slot_assign, slot_row, h, w_up, w_down):
    nf = DFFE // MOE_FC
    grid_spec = pltpu.PrefetchScalarGridSpec(
        num_scalar_prefetch=4,
        grid=(MOE_NT, nf),
        in_specs=[pl.BlockSpec(memory_space=pl.ANY),
                  pl.BlockSpec((1, D, MOE_FC), lambda i, j, te, nv, sa, sr: (te[i], 0, j)),
                  pl.BlockSpec((1, D, MOE_FC), lambda i, j, te, nv, sa, sr: (te[i], 0, nf + j)),
                  pl.BlockSpec((1, MOE_FC, D), lambda i, j, te, nv, sa, sr: (te[i], j, 0))],
        out_specs=pl.BlockSpec(memory_space=pl.ANY),
        scratch_shapes=[pltpu.VMEM((2, MOE_TM, D), F32), pltpu.VMEM((MOE_TM, D), BF16),
                        pltpu.VMEM((2, MOE_TM, D), F32),
                        pltpu.SemaphoreType.DMA((2,)), pltpu.SemaphoreType.DMA((2,))],
    )
    return pl.pallas_call(
        _moe_kernel,
        grid_spec=grid_spec,
        out_shape=jax.ShapeDtypeStruct((2 * R + 2 * MOE_TM, D), F32),
        compiler_params=_cparams(("arbitrary", "arbitrary")),
        name="moe_experts",
    )(tile_expert, tile_rows, slot_assign, slot_row, h, w_up, w_up, w_down)


def _combine_kernel(x_ref, g_ref, e0_ref, e1_ref, wn_ref, y_ref):
    moe = g_ref[:, 0:1] * e0_ref[...] + g_ref[:, 1:2] * e1_ref[...]
    y_ref[...] = _rms(x_ref[...] + moe, wn_ref[...])


def moe_combine_norm(x, gates, eo, wn):
    rows = x.shape[0]
    nb = rows // TM
    return pl.pallas_call(
        _combine_kernel,
        grid=(rows // TM,),
        in_specs=[pl.BlockSpec((TM, D), lambda i: (i, 0)),
                  pl.BlockSpec((TM, HD), lambda i: (i, 0)),
                  pl.BlockSpec((TM, D), lambda i: (i, 0)),
                  pl.BlockSpec((TM, D), lambda i: (nb + i, 0)),
                  pl.BlockSpec((1, D), lambda i: (0, 0))],
        out_specs=pl.BlockSpec((TM, D), lambda i: (i, 0)),
        out_shape=jax.ShapeDtypeStruct((rows, D), F32),
        compiler_params=_cparams(("parallel",)),
        name="moe_combine_norm",
    )(x, gates, eo, eo, wn.reshape(1, D))


def _moe_plan(idx):
    e_flat = idx.reshape(-1)
    onehot = (e_flat[:, None] == jnp.arange(NEXP, dtype=jnp.int32)[None, :]).astype(jnp.int32)
    cum = jnp.cumsum(onehot, axis=0)
    counts = cum[-1]
    padded = ((counts + MOE_TM - 1) // MOE_TM) * MOE_TM
    ends = jnp.cumsum(padded)
    starts = ends - padded
    slot = jnp.sum(onehot * (starts[None, :] + cum - onehot), axis=1)
    pos = jnp.arange(MOE_NT * MOE_TM, dtype=jnp.int32)
    spare = 2 * R + ((pos // MOE_TM) % 2) * MOE_TM + pos % MOE_TM
    flat = jnp.arange(2 * R, dtype=jnp.int32)
    slot_assign = spare.at[slot].set((flat % 2) * R + flat // 2)
    tile_start = jnp.arange(MOE_NT, dtype=jnp.int32) * MOE_TM
    owner = (tile_start[:, None] >= starts[None, :]) & (tile_start[:, None] < ends[None, :])
    tile_rows = jnp.sum(jnp.where(owner, jnp.clip(counts[None, :] - (tile_start[:, None] - starts[None, :]),
                                                  0, MOE_TM), 0), axis=1).astype(jnp.int32)
    last_expert = jnp.max(jnp.where(counts > 0, jnp.arange(NEXP, dtype=jnp.int32), 0))
    tile_expert = jnp.where(jnp.any(owner, axis=1), jnp.argmax(owner, axis=1), last_expert).astype(jnp.int32)
    slot_row = jnp.where(slot_assign < 2 * R, slot_assign % R, 0).astype(jnp.int32)
    return tile_expert, tile_rows, slot_assign, slot_row


def kernel(x_prompt, x_sample, state_hgrn, cache_k, cache_v, page_table, meta_tokens,
           norm_mix_w, norm_ffn_w, hg_w_in, hg_lower_bound, hg_norm_w, hg_w_out,
           at_w_qkv, at_lambda_q1, at_lambda_k1, at_lambda_q2, at_lambda_k2, at_subln_w,
           at_w_out, rel_bias_table, ff_w_up, ff_w_down, moe_w_router, moe_b_router,
           moe_w_up, moe_w_down, final_norm_w):
    xp = jnp.concatenate([jnp.broadcast_to(meta_tokens.astype(F32), (B, NMETA, D)), x_prompt,
                          jnp.zeros((B, LP - LREAL, D), F32)], axis=1).reshape(RP, D)
    xs = x_sample.reshape(RS, D)
    lower = jnp.cumsum(jax.nn.softmax(hg_lower_bound.astype(F32), axis=0), axis=0)

    w_in = hg_w_in[0].astype(BF16)
    y_p, st_p = hgrn_mixer_fused(xp, norm_mix_w[0], w_in, lower[0], hg_norm_w[0],
                                 B, HG_C, LP // HG_C, LREAL)
    proj_s = norm_matmul(xs, norm_mix_w[0], w_in, 1024)
    proj_s = jnp.pad(proj_s.reshape(DB, DT, 4 * D), ((0, 0), (0, HG_CS - DT), (0, 0)))
    y_s, st_s = hgrn_mixer(proj_s.reshape(DB * HG_CS, 4 * D), lower[0], hg_norm_w[0], state_hgrn[0],
                           DB, HG_CS, 1, DT, True, n_seq=HG_NS)
    x1 = matmul_residual(y_p, y_s.reshape(DB, HG_CS, D)[:, :DT].reshape(RS, D),
                         hg_w_out[0].astype(BF16), xp, xs)
    x2 = dense_ffn(x1, norm_ffn_w[0], ff_w_up[0].astype(BF16), ff_w_down[0].astype(BF16))

    f = F32
    lam = (jnp.exp(jnp.sum(at_lambda_q1[0].astype(f) * at_lambda_k1[0].astype(f)))
           - jnp.exp(jnp.sum(at_lambda_q2[0].astype(f) * at_lambda_k2[0].astype(f))) + LAMBDA_INIT)
    lam = lam.reshape(1).astype(F32)
    qkv = norm_matmul(x2, norm_mix_w[1], at_w_qkv[0].astype(BF16), 3 * D)
    q_all, k_all, v_all = qkv[:, :D], qkv[:, D:2 * D], qkv[:, 2 * D:]
    qscale = DQK ** -0.5

    T = AT_T
    dvals = np.arange(-(T - 1), T)
    f_diag = jnp.where(jnp.asarray(dvals >= 0)[None, :], _bias_by_distance(dvals, rel_bias_table), NEG)
    f_prev = _bias_by_distance(dvals + T, rel_bias_table)
    cfar = rel_bias_table[NBUCKETS - 1].astype(F32)
    strips = jnp.stack([_toeplitz(f_diag, T), _toeplitz(f_prev, T),
                        jnp.broadcast_to(cfar[:, None, None], (H, T, T))], axis=1)
    o_p = attn_prompt(qkv, strips, lam, at_subln_w[0])

    q8 = jnp.pad((q_all[RP:] * qscale).reshape(DB, DT, D), ((0, 0), (0, 8 - DT), (0, 0)))
    tq = np.minimum(np.arange(8), DT - 1)
    kpos = np.arange(PAST).reshape(NPAGES, 1, PAGE)
    bp = _bias_by_distance(PAST + tq[None, :, None] - kpos, rel_bias_table)
    bias_pages = jnp.broadcast_to(bp.transpose(1, 0, 2, 3)[:, :, None], (NPAGES, H, 2, 8, PAGE))
    bias_pages = bias_pages.reshape(NPAGES // PP, PP, HD, PAGE).transpose(0, 2, 1, 3)
    bias_pages = bias_pages.reshape(NPAGES // PP, HD, PP * PAGE)
    sn = np.arange(16)
    dn = tq[:, None] - sn[None, :]
    bn = jnp.where(jnp.asarray((dn >= 0) & (sn[None, :] < DT))[None],
                   _bias_by_distance(dn, rel_bias_table), NEG)
    bias_new = jnp.broadcast_to(bn[:, None], (H, 2, 8, 16)).reshape(HD, 16)
    k_new = jnp.pad(k_all[RP:].reshape(DB, DT, D), ((0, 0), (0, 16 - DT), (0, 0)))
    v_new = jnp.pad(v_all[RP:].reshape(DB, DT, D), ((0, 0), (0, 16 - DT), (0, 0)))
    n_phys = cache_k.shape[1]
    cache_kt = jnp.transpose(cache_k[0], (0, 2, 3, 4, 1)).reshape(n_phys * D, PAGE)
    cache_v2 = cache_v[0].reshape(n_phys * PAGE * H, HD)
    o_s = attn_decode(page_table, lam, q8, cache_kt, cache_v2, bias_pages, k_new, v_new, bias_new,
                      at_subln_w[0])
    x3 = matmul_residual(o_p, o_s[:, :DT].reshape(RS, D), at_w_out[0].astype(BF16), x2, None)

    h_moe, idx, gates = moe_router(x3, norm_ffn_w[1], moe_w_router[0], moe_b_router[0])
    tile_expert, tile_rows, slot_assign, slot_row = _moe_plan(idx[:, :TOPK])
    eo = moe_experts(tile_expert, tile_rows, slot_assign, slot_row, h_moe,
                     moe_w_up[0].astype(BF16), moe_w_down[0].astype(BF16))
    y = moe_combine_norm(x3, gates, eo, final_norm_w)

    y_prompt = y[:RP].reshape(B, LP, D)[:, NMETA:LREAL]
    y_sample = y[RP:].reshape(DB, DT, D)
    k_p = k_all[:RP].reshape(B, LP, H, 2, DQK)[:, :LREAL]
    v_p = v_all[:RP].reshape(B, LP, H, HD)[:, :LREAL]
    k_s = k_all[RP:].reshape(DB, DT, H, 2, DQK)
    v_s = v_all[RP:].reshape(DB, DT, H, HD)
    return (y_prompt, y_sample, st_p[None], st_s[None], k_p[None], v_p[None], k_s[None], v_s[None])
```

```python
import functools
import math

import numpy as np
import jax
import jax.numpy as jnp
from jax import lax
from jax.experimental import pallas as pl
from jax.experimental.pallas import tpu as pltpu

F32 = jnp.float32
BF16 = jnp.bfloat16

D = 1024
B = 8
SEQ = 2048
DB = 128
DT = 4
PAST = 2048
PAGE = 128
NPAGES = PAST // PAGE
NMETA = 16
LREAL = NMETA + SEQ
LP = 2176
RP = B * LP
RS = DB * DT
R = RP + RS
H = 8
HD = 128
DQK = 64
NBUCKETS = 32
MAXDIST = 128
DFF = 2816
NEXP = 8
TOPK = 2
DFFE = 3584
EPS = 1e-6
NEG = -1e30
LAMBDA_INIT = 0.8 - 0.6 * math.exp(-0.3 * 1)

VMEM_LIMIT = 56 * 1024 * 1024

TM = 512
HG_C = 128
HG_CS = 16
HG_NS = 4
AT_T = 256
MOE_TM = 512
MOE_NT = (2 * R) // MOE_TM + NEXP
MOE_FC = 1792
FFN_FC = 1408
PP = 8


def _cparams(sem, vmem=VMEM_LIMIT):
    return pltpu.CompilerParams(dimension_semantics=sem, vmem_limit_bytes=vmem)


def _rms(x, w):
    return (x * lax.rsqrt(jnp.mean(x * x, axis=-1, keepdims=True) + EPS)) * w


def _norm_matmul_kernel(x_ref, wn_ref, w_ref, o_ref, h_ref):
    @pl.when(pl.program_id(1) == 0)
    def _():
        h_ref[...] = _rms(x_ref[...], wn_ref[...]).astype(BF16)

    o_ref[...] = jnp.dot(h_ref[...], w_ref[...], preferred_element_type=F32)


def norm_matmul(x, wn, w, tn):
    rows, n = x.shape[0], w.shape[1]
    return pl.pallas_call(
        _norm_matmul_kernel,
        grid=(rows // TM, n // tn),
        in_specs=[pl.BlockSpec((TM, D), lambda i, j: (i, 0)),
                  pl.BlockSpec((1, D), lambda i, j: (0, 0)),
                  pl.BlockSpec((D, tn), lambda i, j: (0, j))],
        out_specs=pl.BlockSpec((TM, tn), lambda i, j: (i, j)),
        out_shape=jax.ShapeDtypeStruct((rows, n), F32),
        scratch_shapes=[pltpu.VMEM((TM, D), BF16)],
        compiler_params=_cparams(("parallel", "arbitrary")),
        name="norm_matmul",
    )(x, wn.reshape(1, D), w)


def _matmul_residual_kernel(ap_ref, as_ref, w_ref, xp_ref, xs_ref, o_ref, *, n_prompt_tiles, one_x):
    def emit(a_ref, x_ref):
        o_ref[...] = x_ref[...] + jnp.dot(a_ref[...].astype(BF16), w_ref[...],
                                          preferred_element_type=F32)

    i = pl.program_id(0)
    pl.when(i < n_prompt_tiles)(lambda: emit(ap_ref, xp_ref))
    pl.when(i >= n_prompt_tiles)(lambda: emit(as_ref, xp_ref if one_x else xs_ref))


def matmul_residual(a_prompt, a_decode, w, x_prompt, x_decode):
    n_p = a_prompt.shape[0] // TM
    n_s = a_decode.shape[0] // TM
    prompt_map = lambda i: (jnp.minimum(i, n_p - 1), 0)
    decode_map = lambda i: (jnp.maximum(i - n_p, 0), 0)
    one_x = x_decode is None
    if one_x:
        x_decode, xp_map, xs_map = x_prompt, (lambda i: (i, 0)), (lambda i: (0, 0))
    else:
        xp_map, xs_map = prompt_map, decode_map
    return pl.pallas_call(
        functools.partial(_matmul_residual_kernel, n_prompt_tiles=n_p, one_x=one_x),
        grid=(n_p + n_s,),
        in_specs=[pl.BlockSpec((TM, D), prompt_map),
                  pl.BlockSpec((TM, D), decode_map),
                  pl.BlockSpec((D, D), lambda i: (0, 0)),
                  pl.BlockSpec((TM, D), xp_map),
                  pl.BlockSpec((TM, D), xs_map)],
        out_specs=pl.BlockSpec((TM, D), lambda i: (i, 0)),
        out_shape=jax.ShapeDtypeStruct(((n_p + n_s) * TM, D), F32),
        compiler_params=_cparams(("parallel",)),
        name="matmul_residual",
    )(a_prompt, a_decode, w, x_prompt, x_decode)


def _ffn_kernel(x_ref, wn_ref, wu_ref, wd_ref, o_ref):
    x = x_ref[...]
    h = _rms(x, wn_ref[...]).astype(BF16)
    acc = x
    for j in range(DFF // FFN_FC):
        a = jnp.dot(h, wu_ref[:, j * FFN_FC:(j + 1) * FFN_FC], preferred_element_type=F32)
        b = jnp.dot(h, wu_ref[:, DFF + j * FFN_FC:DFF + (j + 1) * FFN_FC], preferred_element_type=F32)
        act = (a * jax.nn.sigmoid(a) * b).astype(BF16)
        acc = acc + jnp.dot(act, wd_ref[j * FFN_FC:(j + 1) * FFN_FC, :], preferred_element_type=F32)
    o_ref[...] = acc


def dense_ffn(x, wn, w_up, w_down):
    rows = x.shape[0]
    return pl.pallas_call(
        _ffn_kernel,
        grid=(rows // TM,),
        in_specs=[pl.BlockSpec((TM, D), lambda i: (i, 0)),
                  pl.BlockSpec((1, D), lambda i: (0, 0)),
                  pl.BlockSpec((D, 2 * DFF), lambda i: (0, 0)),
                  pl.BlockSpec((DFF, D), lambda i: (0, 0))],
        out_specs=pl.BlockSpec((TM, D), lambda i: (i, 0)),
        out_shape=jax.ShapeDtypeStruct((rows, D), F32),
        compiler_params=_cparams(("parallel",)),
        name="dense_ffn",
    )(x, wn.reshape(1, D), w_up, w_down)


def _hgrn_masks(C):
    nlev = int(math.log2(C))
    t = np.arange(C)[:, None]
    s = np.arange(C)[None, :]
    m = [(t == s)]
    for lev in range(nlev):
        b = 1 << lev
        m.append((t // (2 * b) == s // (2 * b)) & ((t // b) % 2 == 1) & ((s // b) % 2 == 0))
    return np.stack(m).astype(np.float32)


def _hgrn_kernel(proj_ref, lb_ref, nw_ref, msk_ref, s0_ref, y_ref, s_ref, *, C, n_valid, n_seq):
    for bb in range(n_seq):
        _hgrn_chunk(proj_ref.at[pl.ds(bb * C, C)], lb_ref, nw_ref, msk_ref, s0_ref.at[pl.ds(bb, 1)],
                    y_ref.at[pl.ds(bb * C, C)], s_ref.at[pl.ds(bb, 1)], C, n_valid)


def _hgrn_chunk(proj_ref, lb_ref, nw_ref, msk_ref, s0_ref, y_ref, s_ref, C, n_valid):
    nlev = int(math.log2(C))
    c = pl.program_id(1)

    @pl.when(c == 0)
    def _():
        s_ref[...] = s0_ref[...]

    rloc = lax.broadcasted_iota(jnp.int32, (C, HD), 0)
    valid = (c * C + rloc) < n_valid
    nt = (((1,), (1,)), ((), ()))
    tn = (((0,), (0,)), ((), ()))
    o_heads = []
    for h in range(H):
        col = slice(h * HD, (h + 1) * HD)
        lb = lb_ref[:, col]
        q = proj_ref[:, h * HD:(h + 1) * HD]
        f = proj_ref[:, D + h * HD:D + (h + 1) * HD]
        v = proj_ref[:, 2 * D + h * HD:2 * D + (h + 1) * HD]
        qs = q * jax.nn.sigmoid(q)
        fg = lb + (1.0 - lb) * jax.nn.sigmoid(f)
        k = jnp.where(valid, 1.0 - fg, 0.0)
        g = jnp.where(valid, jnp.log2(fg), 0.0)
        G = g
        sh = 1
        while sh < C:
            G = G + jnp.where(rloc >= sh, pltpu.roll(G, sh, 0), 0.0)
            sh *= 2
        vb = v.astype(BF16)
        A = msk_ref[0] * lax.dot_general(qs.astype(BF16), k.astype(BF16), nt,
                                         preferred_element_type=F32)
        for lev in range(nlev):
            b = 1 << lev
            if b >= 8:
                gm = G.reshape(C // (2 * b), 2 * b, HD)[:, b - 1:b, :]
                gmid = jnp.broadcast_to(gm, (C // (2 * b), 2 * b, HD)).reshape(C, HD)
            else:
                g8 = G.reshape(C // 8, 8, HD)
                sub = lax.broadcasted_iota(jnp.int32, (C // 8, 8, HD), 1)
                gmid = None
                for base in range(0, 8, 2 * b):
                    row = jnp.broadcast_to(g8[:, base + b - 1:base + b, :], (C // 8, 8, HD))
                    gmid = row if gmid is None else jnp.where(sub >= base, row, gmid)
                gmid = gmid.reshape(C, HD)
            right = (rloc // b) % 2 == 1
            fac = jnp.exp2(jnp.where(right, G - gmid, gmid - G))
            qt = jnp.where(right, qs * fac, 0.0).astype(BF16)
            kt = jnp.where(right, 0.0, k * fac).astype(BF16)
            A = A + msk_ref[lev + 1] * lax.dot_general(qt, kt, nt, preferred_element_type=F32)
        g_end = G[C - 1:C, :]
        qh = (qs * jnp.exp2(G)).astype(BF16)
        kh = (k * jnp.exp2(g_end - G)).astype(BF16)
        S = s_ref[0, h]
        o_h = (jnp.dot(A.astype(BF16), vb, preferred_element_type=F32)
               + jnp.dot(qh, S.astype(BF16), preferred_element_type=F32))
        e_col = jnp.broadcast_to(jnp.exp2(g_end), (HD, HD)).T
        s_ref[0, h] = e_col * S + lax.dot_general(kh, vb, tn, preferred_element_type=F32)
        o_heads.append(o_h)
    o = jnp.concatenate(o_heads, axis=1)
    gate = proj_ref[:, 3 * D:4 * D]
    y_ref[...] = _rms(o * jax.nn.sigmoid(gate), nw_ref[...]).astype(BF16)


def _hgrn_fused_kernel(x_ref, xn_ref, wn_ref, w_ref, lb_ref, nw_ref, msk_ref, s0_ref, y_ref, s_ref,
                       proj_scr, *, C, n_valid):
    c = pl.program_id(1)
    slot = lax.rem(c, 2)

    def project(x):
        return jnp.dot(_rms(x, wn_ref[...]).astype(BF16), w_ref[...], preferred_element_type=F32)

    @pl.when(c == 0)
    def _():
        proj_scr[0] = project(x_ref[...])

    proj_scr[1 - slot] = project(xn_ref[...])
    _hgrn_chunk(proj_scr.at[slot], lb_ref, nw_ref, msk_ref, s0_ref, y_ref, s_ref, C, n_valid)


def hgrn_mixer_fused(x, wn, w_in, lb, nw, nb, C, n_chunks, n_valid):
    msk = jnp.asarray(_hgrn_masks(C))
    nlev1 = msk.shape[0]
    const2 = lambda b, c: (0, 0)
    return pl.pallas_call(
        functools.partial(_hgrn_fused_kernel, C=C, n_valid=n_valid),
        grid=(nb, n_chunks),
        in_specs=[pl.BlockSpec((C, D), lambda b, c: (b * n_chunks + c, 0)),
                  pl.BlockSpec((C, D), lambda b, c: (b * n_chunks + jnp.minimum(c + 1, n_chunks - 1), 0)),
                  pl.BlockSpec((1, D), const2),
                  pl.BlockSpec((D, 4 * D), const2),
                  pl.BlockSpec((1, D), const2),
                  pl.BlockSpec((1, D), const2),
                  pl.BlockSpec((nlev1, C, C), lambda b, c: (0, 0, 0)),
                  pl.BlockSpec((1, H, HD, HD), lambda b, c: (0, 0, 0, 0))],
        out_specs=[pl.BlockSpec((C, D), lambda b, c: (b * n_chunks + c, 0)),
                   pl.BlockSpec((1, H, HD, HD), lambda b, c: (b, 0, 0, 0))],
        out_shape=[jax.ShapeDtypeStruct((nb * n_chunks * C, D), BF16),
                   jax.ShapeDtypeStruct((nb, H, HD, HD), F32)],
        scratch_shapes=[pltpu.VMEM((2, C, 4 * D), F32)],
        compiler_params=_cparams(("parallel", "arbitrary")),
        name="hgrn_mixer_fused",
    )(x, x, wn.reshape(1, D), w_in, lb.reshape(1, D), nw.reshape(1, D), msk,
      jnp.zeros((1, H, HD, HD), F32))


def hgrn_mixer(proj, lb, nw, s0, nb, C, n_chunks, n_valid, s0_per_batch, n_seq=1):
    assert n_seq == 1 or (n_chunks == 1 and s0_per_batch and nb % n_seq == 0)
    msk = jnp.asarray(_hgrn_masks(C))
    nlev1 = msk.shape[0]
    s0_map = (lambda b, c: (b, 0, 0, 0)) if s0_per_batch else (lambda b, c: (0, 0, 0, 0))
    return pl.pallas_call(
        functools.partial(_hgrn_kernel, C=C, n_valid=n_valid, n_seq=n_seq),
        grid=(nb // n_seq, n_chunks),
        in_specs=[pl.BlockSpec((n_seq * C, 4 * D), lambda b, c: (b * n_chunks + c, 0)),
                  pl.BlockSpec((1, D), lambda b, c: (0, 0)),
                  pl.BlockSpec((1, D), lambda b, c: (0, 0)),
                  pl.BlockSpec((nlev1, C, C), lambda b, c: (0, 0, 0)),
                  pl.BlockSpec((n_seq, H, HD, HD), s0_map)],
        out_specs=[pl.BlockSpec((n_seq * C, D), lambda b, c: (b * n_chunks + c, 0)),
                   pl.BlockSpec((n_seq, H, HD, HD), lambda b, c: (b, 0, 0, 0))],
        out_shape=[jax.ShapeDtypeStruct((nb * n_chunks * C, D), BF16),
                   jax.ShapeDtypeStruct((nb, H, HD, HD), F32)],
        compiler_params=_cparams(("parallel", "arbitrary")),
        name="hgrn_mixer",
    )(proj, lb.reshape(1, D), nw.reshape(1, D), msk, s0)


def _bucket_of_distance(dist):
    n = np.maximum(np.asarray(dist), 0)
    max_exact = NBUCKETS // 2
    nf = np.maximum(n, 1).astype(np.float64)
    large = max_exact + (np.log(nf / max_exact) / math.log(MAXDIST / max_exact)
                         * (NBUCKETS - max_exact)).astype(np.int64)
    return np.where(n < max_exact, n, np.minimum(large, NBUCKETS - 1))


def _bias_by_distance(dist, table):
    dist = np.asarray(dist)
    onehot = np.eye(NBUCKETS, dtype=np.float32)[_bucket_of_distance(dist).reshape(-1)]
    out = jnp.einsum("nb,bh->hn", jnp.asarray(onehot), table.astype(F32),
                     precision=lax.Precision.HIGHEST)
    return out.reshape((H,) + dist.shape)


def _toeplitz(fvals, T):
    w = jnp.pad(fvals[:, ::-1], ((0, 0), (0, 1)))
    flat = jnp.tile(w, (1, T))[:, :T * (2 * T - 1)]
    return flat.reshape(H, T, 2 * T - 1)[:, :, T - 1:]


def _subln(d, w):
    return _rms(d, w) * (1.0 - LAMBDA_INIT)


def _attn_prompt_kernel(lam_ref, q_ref, k_ref, v_ref, strip_ref, sw_ref, o_ref, s_scr, m_scr, kb_scr, vx_scr):
    T = AT_T
    starts = list(range(0, LP, T))
    sizes = [min(T, LP - r0) for r0 in starts]
    nq = len(starts)
    nt = (((1,), (1,)), ((), ()))
    kb_scr[...] = k_ref[...].astype(BF16)
    vx_scr[:, :HD] = v_ref[...].astype(BF16)
    vx_scr[:, HD:] = jnp.ones((LP, HD), BF16)

    def step(t):
        slot = t % 2
        if t < nq:
            tq = sizes[t]
            qb = (q_ref[starts[t]:starts[t] + tq, :] * (DQK ** -0.5)).astype(BF16)
            lane = lax.broadcasted_iota(jnp.int32, (tq, HD), 1)
            zero = jnp.zeros_like(qb)
            qq = jnp.concatenate([jnp.where(lane < DQK, qb, zero), jnp.where(lane >= DQK, qb, zero)], axis=0)
            mx = jnp.full((2 * tq, HD), NEG, F32)
        if t > 0:
            tp = sizes[t - 1]
            m_prev = m_scr[1 - slot, :2 * tp, :]
            acc = jnp.zeros((2 * tp, 2 * HD), F32)
        for c in range(min(t + 1, nq)):
            tk = sizes[c]
            if t < nq:
                bt = strip_ref[0, min(t - c, 2), :tq, :tk]
                sc = lax.dot_general(qq, kb_scr[starts[c]:starts[c] + tk, :], nt,
                                     preferred_element_type=F32) + jnp.concatenate([bt, bt], axis=0)
                s_scr[slot, c, :2 * tq, :tk] = sc
                for u in range(tk // HD):
                    mx = jnp.maximum(mx, sc[:, u * HD:(u + 1) * HD])
            if t > 0 and c < t:
                sp = s_scr[1 - slot, c, :2 * tp, :tk]
                p = jnp.exp(jnp.concatenate([sp[:, u * HD:(u + 1) * HD] - m_prev for u in range(tk // HD)],
                                            axis=1))
                acc = acc + jnp.dot(p.astype(BF16), vx_scr[starts[c]:starts[c] + tk, :],
                                    preferred_element_type=F32)
        if t < nq:
            m_scr[slot, :2 * tq, :] = jnp.broadcast_to(jnp.max(mx, axis=-1, keepdims=True), (2 * tq, HD))
        if t > 0:
            o = acc[:, :HD] / acc[:, HD:]
            o_ref[starts[t - 1]:starts[t - 1] + tp, :] = _subln(o[:tp] - lam_ref[0] * o[tp:], sw_ref[...])

    for t in range(nq + 1):
        step(t)


def attn_prompt(qkv, strips, lam, sw):
    T = AT_T
    nq = -(-LP // T)
    smem = pl.BlockSpec(memory_space=pltpu.SMEM)
    return pl.pallas_call(
        _attn_prompt_kernel,
        grid=(B, H),
        in_specs=[smem,
                  pl.BlockSpec((LP, HD), lambda b, h: (b, h)),
                  pl.BlockSpec((LP, HD), lambda b, h: (b, H + h)),
                  pl.BlockSpec((LP, HD), lambda b, h: (b, 2 * H + h)),
                  pl.BlockSpec((1, 3, T, T), lambda b, h: (h, 0, 0, 0)),
                  pl.BlockSpec((1, HD), lambda b, h: (0, 0))],
        out_specs=pl.BlockSpec((LP, HD), lambda b, h: (b, h)),
        out_shape=jax.ShapeDtypeStruct((B * LP, D), F32),
        scratch_shapes=[pltpu.VMEM((2, nq, 2 * T, T), F32), pltpu.VMEM((2, 2 * T, HD), F32),
                        pltpu.VMEM((LP, HD), BF16), pltpu.VMEM((LP, 2 * HD), BF16)],
        compiler_params=_cparams(("parallel", "parallel")),
        name="attn_prompt",
    )(lam, qkv, qkv, qkv, strips, sw.reshape(1, HD))


def _attn_decode_kernel(pt_ref, lam_ref, q_ref, *refs):
    del pt_ref
    k_refs = refs[:PP]
    v_refs = refs[PP:2 * PP]
    bias_ref, kn_ref, vn_ref, bn_ref, sw_ref, o_ref, m_ref, acc_ref, q_scr = refs[2 * PP:]
    g = pl.program_id(1)
    last = pl.num_programs(1) - 1
    nt = (((1,), (1,)), ((), ()))

    @pl.when(g == 0)
    def _():
        m_ref[...] = jnp.full(m_ref.shape, NEG, F32)
        acc_ref[...] = jnp.zeros(acc_ref.shape, F32)
        q8 = q_ref[0]
        r = lax.broadcasted_iota(jnp.int32, (HD, D), 0)
        col = lax.broadcasted_iota(jnp.int32, (HD, D), 1)
        q_scr[...] = jnp.where(r // 8 == col // DQK, jnp.concatenate([q8] * 16, axis=0), 0.0).astype(BF16)

    qrows = q_scr[...]
    kt = jnp.concatenate([k_refs[j][...].astype(BF16) for j in range(PP)], axis=1)
    s = jnp.dot(qrows, kt, preferred_element_type=F32) + bias_ref[0]
    ones = jnp.ones((PP * PAGE, HD), BF16)

    def finish(s_new):
        smax = s[:, :PAGE]
        for j in range(1, PP):
            smax = jnp.maximum(smax, s[:, j * PAGE:(j + 1) * PAGE])
        m_cur = jnp.max(smax, axis=-1, keepdims=True)
        if s_new is not None:
            m_cur = jnp.maximum(m_cur, jnp.max(s_new, axis=-1, keepdims=True))
        m_old = m_ref[...]
        m_new = jnp.maximum(m_old, jnp.broadcast_to(m_cur, (HD, HD)))
        m_ref[...] = m_new
        alpha = jnp.exp(m_old - m_new)
        alpha2 = jnp.concatenate([alpha, alpha], axis=1)
        p = jnp.exp(s - jnp.concatenate([m_new] * PP, axis=1)).astype(BF16)
        if s_new is not None:
            p_new = jnp.exp(s_new - m_new[:, :16]).astype(BF16)
        for hh in range(H):
            rows = slice(hh * 16, (hh + 1) * 16)
            vh = jnp.concatenate([v_refs[j][pl.ds(hh, PAGE, stride=H), :].astype(BF16)
                                  for j in range(PP)], axis=0)
            pv = jnp.dot(p[rows], jnp.concatenate([vh, ones], axis=1), preferred_element_type=F32)
            if s_new is not None:
                vn = vn_ref[0, :, hh * HD:(hh + 1) * HD].astype(BF16)
                pv = pv + jnp.dot(p_new[rows], jnp.concatenate([vn, ones[:16]], axis=1),
                                  preferred_element_type=F32)
            acc_ref[rows, :] = alpha2[rows] * acc_ref[rows, :] + pv

    @pl.when(g < last)
    def _():
        finish(None)

    @pl.when(g == last)
    def _():
        finish(lax.dot_general(qrows, kn_ref[0].astype(BF16), nt,
                               preferred_element_type=F32) + bn_ref[...])
        acc = acc_ref[...]
        o = acc[:, :HD] / acc[:, HD:]
        outs = []
        for hh in range(H):
            o1 = o[hh * 16:hh * 16 + 8]
            o2 = o[hh * 16 + 8:hh * 16 + 16]
            outs.append(_subln(o1 - lam_ref[0] * o2, sw_ref[...]))
        o_ref[0] = jnp.concatenate(outs, axis=1)


def attn_decode(page_table, lam, q8, cache_kt, cache_v2, bias_pages, k_new, v_new, bias_new, sw):
    n_steps = NPAGES // PP

    def page_spec(j):
        return pl.BlockSpec((D, PAGE), lambda b, g, pt: (pt[b * NPAGES + g * PP + j], 0))

    smem = pl.BlockSpec(memory_space=pltpu.SMEM)
    grid_spec = pltpu.PrefetchScalarGridSpec(
        num_scalar_prefetch=1,
        grid=(DB, n_steps),
        in_specs=([smem, pl.BlockSpec((1, 8, D), lambda b, g, pt: (b, 0, 0))]
                  + [page_spec(j) for j in range(PP)]
                  + [page_spec(j) for j in range(PP)]
                  + [pl.BlockSpec((1, HD, PP * PAGE), lambda b, g, pt: (g, 0, 0)),
                     pl.BlockSpec((1, 16, D), lambda b, g, pt: (b, 0, 0)),
                     pl.BlockSpec((1, 16, D), lambda b, g, pt: (b, 0, 0)),
                     pl.BlockSpec((HD, 16), lambda b, g, pt: (0, 0)),
                     pl.BlockSpec((1, HD), lambda b, g, pt: (0, 0))]),
        out_specs=pl.BlockSpec((1, 8, D), lambda b, g, pt: (b, 0, 0)),
        scratch_shapes=[pltpu.VMEM((HD, HD), F32), pltpu.VMEM((HD, 2 * HD), F32),
                        pltpu.VMEM((HD, D), BF16)],
    )
    return pl.pallas_call(
        _attn_decode_kernel,
        grid_spec=grid_spec,
        out_shape=jax.ShapeDtypeStruct((DB, 8, D), F32),
        compiler_params=_cparams(("parallel", "arbitrary")),
        name="attn_decode",
    )(page_table.reshape(-1), lam, q8, *([cache_kt] * PP), *([cache_v2] * PP),
      bias_pages, k_new, v_new, bias_new, sw.reshape(1, HD))


def _router_kernel(x_ref, wn_ref, wr_ref, br_ref, h_ref, idx_ref, gate_ref):
    h = _rms(x_ref[...], wn_ref[...])
    h_ref[...] = h
    h_hi = h.astype(BF16)
    h_lo = (h - h_hi.astype(F32)).astype(BF16)
    logits = (jnp.dot(h_hi, wr_ref[0], preferred_element_type=F32)
              + jnp.dot(h_lo, wr_ref[0], preferred_element_type=F32)
              + jnp.dot(h_hi, wr_ref[1], preferred_element_type=F32)) + br_ref[...]
    lane = lax.broadcasted_iota(jnp.int32, logits.shape, 1)
    ninf = jnp.float32(-jnp.inf)
    logits = jnp.where(lane < NEXP, logits, ninf)
    v1 = jnp.max(logits, axis=-1, keepdims=True)
    i1 = jnp.min(jnp.where(logits == v1, lane, HD), axis=-1, keepdims=True)
    rest = jnp.where(lane == i1, ninf, logits)
    v2 = jnp.max(rest, axis=-1, keepdims=True)
    i2 = jnp.min(jnp.where(rest == v2, lane, HD), axis=-1, keepdims=True)
    e = jnp.exp(v2 - v1)
    g1 = 1.0 / (1.0 + e)
    g2 = e / (1.0 + e)
    idx_ref[...] = jnp.where(lane == 0, i1, jnp.where(lane == 1, i2, 0))
    gate_ref[...] = jnp.where(lane == 0, g1, jnp.where(lane == 1, g2, 0.0))


def moe_router(x, wn, w_router, b_router):
    rows = x.shape[0]
    wr = jnp.zeros((D, HD), F32).at[:, :NEXP].set(w_router.astype(F32))
    wr_hi = wr.astype(BF16)
    wr_lo = (wr - wr_hi.astype(F32)).astype(BF16)
    br = jnp.zeros((1, HD), F32).at[0, :NEXP].set(b_router.astype(F32))
    return pl.pallas_call(
        _router_kernel,
        grid=(rows // TM,),
        in_specs=[pl.BlockSpec((TM, D), lambda i: (i, 0)),
                  pl.BlockSpec((1, D), lambda i: (0, 0)),
                  pl.BlockSpec((2, D, HD), lambda i: (0, 0, 0)),
                  pl.BlockSpec((1, HD), lambda i: (0, 0))],
        out_specs=[pl.BlockSpec((TM, D), lambda i: (i, 0)),
                   pl.BlockSpec((TM, HD), lambda i: (i, 0)),
                   pl.BlockSpec((TM, HD), lambda i: (i, 0))],
        out_shape=[jax.ShapeDtypeStruct((rows, D), F32),
                   jax.ShapeDtypeStruct((rows, HD), jnp.int32),
                   jax.ShapeDtypeStruct((rows, HD), F32)],
        compiler_params=_cparams(("parallel",)),
        name="moe_router",
    )(x, wn.reshape(1, D), jnp.stack([wr_hi, wr_lo]), br)


def _moe_kernel(te_ref, nv_ref, sa_ref, sr_ref, h_hbm, wa_ref, wb_ref, wd_ref, out_hbm,
                hbuf, obuf, gsem, ssem):
    del te_ref
    i = pl.program_id(0)
    nt = pl.num_programs(0)
    slot = lax.rem(i, 2)
    live = nv_ref[i] > 0

    def start_gather(tile, s):
        def body(g, carry):
            base = pl.multiple_of(g * 8, 8)
            for u in range(8):
                row = sr_ref[tile * MOE_TM + base + u]
                pltpu.make_async_copy(h_hbm.at[pl.ds(row, 1)], hbuf.at[s, pl.ds(base + u, 1)],
                                      gsem.at[s]).start()
            return carry
        lax.fori_loop(0, MOE_TM // 8, body, 0)

    def wait_gather(s):
        pltpu.make_async_copy(h_hbm.at[pl.ds(0, MOE_TM)], hbuf.at[s], gsem.at[s]).wait()

    def start_scatter(tile, s):
        def body(g, carry):
            base = pl.multiple_of(g * 8, 8)
            for u in range(8):
                a = sa_ref[tile * MOE_TM + base + u]
                pltpu.make_async_copy(obuf.at[s, pl.ds(base + u, 1)], out_hbm.at[pl.ds(a, 1)],
                                      ssem.at[s]).start()
            return carry
        lax.fori_loop(0, MOE_TM // 8, body, 0)

    def wait_scatter(s):
        pltpu.make_async_copy(obuf.at[s], out_hbm.at[pl.ds(0, MOE_TM)], ssem.at[s]).wait()

    @pl.when((i >= 2) & (nv_ref[jnp.maximum(i - 2, 0)] > 0))
    def _():
        wait_scatter(slot)

    @pl.when(live)
    def _():
        @pl.when(i == 0)
        def _():
            start_gather(0, 0)
            obuf[1] = jnp.zeros((MOE_TM, D), F32)
            for blk in range(2):
                pltpu.make_async_copy(obuf.at[1], out_hbm.at[pl.ds(2 * R + blk * MOE_TM, MOE_TM)],
                                      ssem.at[1]).start()
            for blk in range(2):
                wait_scatter(1)

        nxt = jnp.minimum(i + 1, nt - 1)

        @pl.when((i + 1 < nt) & (nv_ref[nxt] > 0))
        def _():
            start_gather(nxt, 1 - slot)

        wait_gather(slot)
        h = hbuf[slot].astype(BF16)
        y = None
        for j in range(DFFE // MOE_FC):
            a = jnp.dot(h, wa_ref[0, :, j * MOE_FC:(j + 1) * MOE_FC], preferred_element_type=F32)
            b = jnp.dot(h, wb_ref[0, :, j * MOE_FC:(j + 1) * MOE_FC], preferred_element_type=F32)
            act = (a * jax.nn.sigmoid(a) * b).astype(BF16)
            yj = jnp.dot(act, wd_ref[0, j * MOE_FC:(j + 1) * MOE_FC, :], preferred_element_type=F32)
            y = yj if y is None else y + yj
        obuf[slot] = y
        start_scatter(i, slot)

    @pl.when(i == nt - 1)
    def _():
        @pl.when(nv_ref[nt - 2] > 0)
        def _():
            wait_scatter(1 - slot)

        @pl.when(live)
        def _():
            wait_scatter(slot)


def moe_experts(tile_expert, tile_rows, slot_assign, slot_row, h, w_up, w_down):
    once = pl.Buffered(1)
    grid_spec = pltpu.PrefetchScalarGridSpec(
        num_scalar_prefetch=4,
        grid=(MOE_NT,),
        in_specs=[pl.BlockSpec(memory_space=pl.ANY),
                  pl.BlockSpec((1, D, DFFE), lambda i, te, nv, sa, sr: (te[i], 0, 0), pipeline_mode=once),
                  pl.BlockSpec((1, D, DFFE), lambda i, te, nv, sa, sr: (te[i], 0, 1), pipeline_mode=once),
                  pl.BlockSpec((1, DFFE, D), lambda i, te, nv, sa, sr: (te[i], 0, 0), pipeline_mode=once)],
        out_specs=pl.BlockSpec(memory_space=pl.ANY),
        scratch_shapes=[pltpu.VMEM((2, MOE_TM, D), F32),
                        pltpu.VMEM((2, MOE_TM, D), F32),
                        pltpu.SemaphoreType.DMA((2,)), pltpu.SemaphoreType.DMA((2,))],
    )
    return pl.pallas_call(
        _moe_kernel,
        grid_spec=grid_spec,
        out_shape=jax.ShapeDtypeStruct((2 * R + 2 * MOE_TM, D), F32),
        compiler_params=_cparams(("arbitrary",)),
        name="moe_experts",
    )(tile_expert, tile_rows, slot_assign, slot_row, h, w_up, w_up, w_down)


def _combine_kernel(x_ref, g_ref, e0_ref, e1_ref, wn_ref, y_ref):
    moe = g_ref[:, 0:1] * e0_ref[...] + g_ref[:, 1:2] * e1_ref[...]
    y_ref[...] = _rms(x_ref[...] + moe, wn_ref[...])


def moe_combine_norm(x, gates, eo, wn):
    rows = x.shape[0]
    nb = rows // TM
    return pl.pallas_call(
        _combine_kernel,
        grid=(rows // TM,),
        in_specs=[pl.BlockSpec((TM, D), lambda i: (i, 0)),
                  pl.BlockSpec((TM, HD), lambda i: (i, 0)),
                  pl.BlockSpec((TM, D), lambda i: (i, 0)),
                  pl.BlockSpec((TM, D), lambda i: (nb + i, 0)),
                  pl.BlockSpec((1, D), lambda i: (0, 0))],
        out_specs=pl.BlockSpec((TM, D), lambda i: (i, 0)),
        out_shape=jax.ShapeDtypeStruct((rows, D), F32),
        compiler_params=_cparams(("parallel",)),
        name="moe_combine_norm",
    )(x, gates, eo, eo, wn.reshape(1, D))


def _moe_plan(idx):
    e_flat = idx.reshape(-1)
    onehot = (e_flat[:, None] == jnp.arange(NEXP, dtype=jnp.int32)[None, :]).astype(jnp.int32)
    cum = jnp.cumsum(onehot, axis=0)
    counts = cum[-1]
    padded = ((counts + MOE_TM - 1) // MOE_TM) * MOE_TM
    ends = jnp.cumsum(padded)
    starts = ends - padded
    slot = jnp.sum(onehot * (starts[None, :] + cum - onehot), axis=1)
    pos = jnp.arange(MOE_NT * MOE_TM, dtype=jnp.int32)
    spare = 2 * R + ((pos // MOE_TM) % 2) * MOE_TM + pos % MOE_TM
    flat = jnp.arange(2 * R, dtype=jnp.int32)
    slot_assign = spare.at[slot].set((flat % 2) * R + flat // 2)
    tile_start = jnp.arange(MOE_NT, dtype=jnp.int32) * MOE_TM
    owner = (tile_start[:, None] >= starts[None, :]) & (tile_start[:, None] < ends[None, :])
    tile_rows = jnp.sum(jnp.where(owner, jnp.clip(counts[None, :] - (tile_start[:, None] - starts[None, :]),
                                                  0, MOE_TM), 0), axis=1).astype(jnp.int32)
    last_expert = jnp.max(jnp.where(counts > 0, jnp.arange(NEXP, dtype=jnp.int32), 0))
    tile_expert = jnp.where(jnp.any(owner, axis=1), jnp.argmax(owner, axis=1), last_expert).astype(jnp.int32)
    slot_row = jnp.where(slot_assign < 2 * R, slot_assign % R, 0).astype(jnp.int32)
    return tile_expert, tile_rows, slot_assign, slot_row


def kernel(x_prompt, x_sample, state_hgrn, cache_k, cache_v, page_table, meta_tokens,
           norm_mix_w, norm_ffn_w, hg_w_in, hg_lower_bound, hg_norm_w, hg_w_out,
           at_w_qkv, at_lambda_q1, at_lambda_k1, at_lambda_q2, at_lambda_k2, at_subln_w,
           at_w_out, rel_bias_table, ff_w_up, ff_w_down, moe_w_router, moe_b_router,
           moe_w_up, moe_w_down, final_norm_w):
    xp = jnp.concatenate([jnp.broadcast_to(meta_tokens.astype(F32), (B, NMETA, D)), x_prompt,
                          jnp.zeros((B, LP - LREAL, D), F32)], axis=1).reshape(RP, D)
    xs = x_sample.reshape(RS, D)
    lower = jnp.cumsum(jax.nn.softmax(hg_lower_bound.astype(F32), axis=0), axis=0)

    w_in = hg_w_in[0].astype(BF16)
    y_p, st_p = hgrn_mixer_fused(xp, norm_mix_w[0], w_in, lower[0], hg_norm_w[0],
                                 B, HG_C, LP // HG_C, LREAL)
    proj_s = norm_matmul(xs, norm_mix_w[0], w_in, 1024)
    proj_s = jnp.pad(proj_s.reshape(DB, DT, 4 * D), ((0, 0), (0, HG_CS - DT), (0, 0)))
    y_s, st_s = hgrn_mixer(proj_s.reshape(DB * HG_CS, 4 * D), lower[0], hg_norm_w[0], state_hgrn[0],
                           DB, HG_CS, 1, DT, True, n_seq=HG_NS)
    x1 = matmul_residual(y_p, y_s.reshape(DB, HG_CS, D)[:, :DT].reshape(RS, D),
                         hg_w_out[0].astype(BF16), xp, xs)
    x2 = dense_ffn(x1, norm_ffn_w[0], ff_w_up[0].astype(BF16), ff_w_down[0].astype(BF16))

    f = F32
    lam = (jnp.exp(jnp.sum(at_lambda_q1[0].astype(f) * at_lambda_k1[0].astype(f)))
           - jnp.exp(jnp.sum(at_lambda_q2[0].astype(f) * at_lambda_k2[0].astype(f))) + LAMBDA_INIT)
    lam = lam.reshape(1).astype(F32)
    qkv = norm_matmul(x2, norm_mix_w[1], at_w_qkv[0].astype(BF16), 3 * D)
    q_all, k_all, v_all = qkv[:, :D], qkv[:, D:2 * D], qkv[:, 2 * D:]
    qscale = DQK ** -0.5

    T = AT_T
    dvals = np.arange(-(T - 1), T)
    f_diag = jnp.where(jnp.asarray(dvals >= 0)[None, :], _bias_by_distance(dvals, rel_bias_table), NEG)
    f_prev = _bias_by_distance(dvals + T, rel_bias_table)
    cfar = rel_bias_table[NBUCKETS - 1].astype(F32)
    strips = jnp.stack([_toeplitz(f_diag, T), _toeplitz(f_prev, T),
                        jnp.broadcast_to(cfar[:, None, None], (H, T, T))], axis=1)
    o_p = attn_prompt(qkv, strips, lam, at_subln_w[0])

    q8 = jnp.pad((q_all[RP:] * qscale).reshape(DB, DT, D), ((0, 0), (0, 8 - DT), (0, 0)))
    tq = np.minimum(np.arange(8), DT - 1)
    kpos = np.arange(PAST).reshape(NPAGES, 1, PAGE)
    bp = _bias_by_distance(PAST + tq[None, :, None] - kpos, rel_bias_table)
    bias_pages = jnp.broadcast_to(bp.transpose(1, 0, 2, 3)[:, :, None], (NPAGES, H, 2, 8, PAGE))
    bias_pages = bias_pages.reshape(NPAGES // PP, PP, HD, PAGE).transpose(0, 2, 1, 3)
    bias_pages = bias_pages.reshape(NPAGES // PP, HD, PP * PAGE)
    sn = np.arange(16)
    dn = tq[:, None] - sn[None, :]
    bn = jnp.where(jnp.asarray((dn >= 0) & (sn[None, :] < DT))[None],
                   _bias_by_distance(dn, rel_bias_table), NEG)
    bias_new = jnp.broadcast_to(bn[:, None], (H, 2, 8, 16)).reshape(HD, 16)
    k_new = jnp.pad(k_all[RP:].reshape(DB, DT, D), ((0, 0), (0, 16 - DT), (0, 0)))
    v_new = jnp.pad(v_all[RP:].reshape(DB, DT, D), ((0, 0), (0, 16 - DT), (0, 0)))
    n_phys = cache_k.shape[1]
    cache_kt = jnp.transpose(cache_k[0], (0, 2, 3, 4, 1)).reshape(n_phys * D, PAGE)
    cache_v2 = cache_v[0].reshape(n_phys * PAGE * H, HD)
    o_s = attn_decode(page_table, lam, q8, cache_kt, cache_v2, bias_pages, k_new, v_new, bias_new,
                      at_subln_w[0])
    x3 = matmul_residual(o_p, o_s[:, :DT].reshape(RS, D), at_w_out[0].astype(BF16), x2, None)

    h_moe, idx, gates = moe_router(x3, norm_ffn_w[1], moe_w_router[0], moe_b_router[0])
    tile_expert, tile_rows, slot_assign, slot_row = _moe_plan(idx[:, :TOPK])
    eo = moe_experts(tile_expert, tile_rows, slot_assign, slot_row, h_moe,
                     moe_w_up[0].astype(BF16), moe_w_down[0].astype(BF16))
    y = moe_combine_norm(x3, gates, eo, final_norm_w)

    y_prompt = y[:RP].reshape(B, LP, D)[:, NMETA:LREAL]
    y_sample = y[RP:].reshape(DB, DT, D)
    k_p = k_all[:RP].reshape(B, LP, H, 2, DQK)[:, :LREAL]
    v_p = v_all[:RP].reshape(B, LP, H, HD)[:, :LREAL]
    k_s = k_all[RP:].reshape(DB, DT, H, 2, DQK)
    v_s = v_all[RP:].reshape(DB, DT, H, HD)
    return (y_prompt, y_sample, st_p[None], st_s[None], k_p[None], v_p[None], k_s[None], v_s[None])
```
